```python
import math
import jax, jax.numpy as jnp
from jax import lax
import numpy as np

D_MODEL = 1024
BATCH = 8
SEQ = 2048
DEPTH = 2

MLA_HEADS = 6
MLA_Q_RANK = 384
MLA_KV_RANK = 256
MLA_NOPE = 64
MLA_ROPE = 32
MLA_QK = MLA_NOPE + MLA_ROPE
MLA_V = 64
DIFF_HEADS = 4
DIFF_QK = 64
DIFF_V = 2 * DIFF_QK
GQA_HEADS = 6
GQA_KV_HEADS = 2
GQA_GROUP = GQA_HEADS // GQA_KV_HEADS
GQA_DIM = 64
GRID_W = 64
Q_BLOCK = 128
ROPE_THETA = 10000.0
EPS = 1e-6
FFN_HIDDEN = -(-(8 * D_MODEL) // (3 * 256)) * 256

IN_SIZES = [
    MLA_Q_RANK, MLA_KV_RANK, MLA_ROPE,
    DIFF_HEADS * 2 * DIFF_QK, DIFF_HEADS * 2 * DIFF_QK, DIFF_HEADS * DIFF_V,
    GQA_HEADS * GQA_DIM, GQA_KV_HEADS * GQA_DIM, GQA_KV_HEADS * GQA_DIM,
]
IN_WIDTH = sum(IN_SIZES)
IN_OFFSETS = [int(v) for v in np.cumsum(IN_SIZES)[:-1]]
MIX_WIDTH = MLA_HEADS * MLA_V + DIFF_HEADS * DIFF_V + GQA_HEADS * GQA_DIM

kernel_name = "hybrid_mla_diff_axialgqa_encoder"


def _rms_norm(x, gain):
    xf = x.astype(jnp.float32)
    y = xf * lax.rsqrt(jnp.mean(xf * xf, axis=-1, keepdims=True) + EPS)
    return (y * gain.astype(jnp.float32)).astype(x.dtype)


def _rope_cos_sin(pos, dim):
    inv = 1.0 / (ROPE_THETA ** (jnp.arange(0, dim, 2, dtype=jnp.float32) / dim))
    ang = pos.astype(jnp.float32)[:, None] * inv[None, :]
    return jnp.cos(ang), jnp.sin(ang)


def _apply_rope(x, cos, sin):
    xf = x.astype(jnp.float32)
    half = x.shape[-1] // 2
    x1, x2 = xf[..., :half], xf[..., half:]
    return jnp.concatenate([x1 * cos - x2 * sin, x1 * sin + x2 * cos], axis=-1).astype(x.dtype)


def _sweep_query_blocks(block_fn, q):
    s_len = q.shape[-2]
    nb = s_len // Q_BLOCK
    qb = q.reshape(q.shape[:-2] + (nb, Q_BLOCK, q.shape[-1]))
    qb = jnp.moveaxis(qb, -3, 0)
    out = lax.map(block_fn, qb)
    out = jnp.moveaxis(out, 0, -3)
    return out.reshape(out.shape[:-3] + (s_len, out.shape[-1]))


def _dense_attention(q, k, v, scale, score_eq, out_eq):
    def block_fn(qb):
        s = jnp.einsum(score_eq, qb, k).astype(jnp.float32) * scale
        p = jax.nn.softmax(s, axis=-1)
        return jnp.einsum(out_eq, p.astype(v.dtype), v)
    return _sweep_query_blocks(block_fn, q)


def _normal(key, shape, scale):
    return jax.random.normal(key, shape, jnp.float32) * scale


def _gain(key, shape):
    return 1.0 + 0.02 * jax.random.normal(key, shape, jnp.float32)


def setup_inputs(seed: int = 0) -> dict:
    key = jax.random.key(seed)
    ks = jax.random.split(key, 24)
    L = DEPTH
    return {
        "x": jax.random.normal(ks[0], (BATCH, SEQ, D_MODEL), jnp.float32),
        "attn_norm": _gain(ks[1], (L, D_MODEL)),
        "w_in": _normal(ks[2], (L, D_MODEL, IN_WIDTH), D_MODEL ** -0.5),
        "mla_q_norm": _gain(ks[3], (L, MLA_Q_RANK)),
        "mla_w_uq": _normal(ks[4], (L, MLA_Q_RANK, MLA_HEADS * MLA_QK), MLA_Q_RANK ** -0.5),
        "mla_kv_norm": _gain(ks[5], (L, MLA_KV_RANK)),
        "mla_w_ukv": _normal(ks[6], (L, MLA_KV_RANK, MLA_HEADS * (MLA_NOPE + MLA_V)), MLA_KV_RANK ** -0.5),
        "mla_q_gain": _gain(ks[7], (L, MLA_QK)),
        "mla_k_gain": _gain(ks[8], (L, MLA_QK)),
        "diff_q_gain": _gain(ks[9], (L, DIFF_QK)),
        "diff_k_gain": _gain(ks[10], (L, DIFF_QK)),
        "diff_lq1": _normal(ks[11], (L, DIFF_QK), 0.1),
        "diff_lk1": _normal(ks[12], (L, DIFF_QK), 0.1),
        "diff_lq2": _normal(ks[13], (L, DIFF_QK), 0.1),
        "diff_lk2": _normal(ks[14], (L, DIFF_QK), 0.1),
        "diff_out_gain": _gain(ks[15], (L, DIFF_V)),
        "gqa_q_gain": _gain(ks[16], (L, GQA_DIM)),
        "gqa_k_gain": _gain(ks[17], (L, GQA_DIM)),
        "w_o": _normal(ks[18], (L, MIX_WIDTH, D_MODEL), MIX_WIDTH ** -0.5),
        "ffn_norm": _gain(ks[19], (L, D_MODEL)),
        "w_gate_up": _normal(ks[20], (L, D_MODEL, 2 * FFN_HIDDEN), D_MODEL ** -0.5),
        "w_down": _normal(ks[21], (L, FFN_HIDDEN, D_MODEL), FFN_HIDDEN ** -0.5),
    }


def reference(x, attn_norm, w_in, mla_q_norm, mla_w_uq, mla_kv_norm, mla_w_ukv,
              mla_q_gain, mla_k_gain, diff_q_gain, diff_k_gain, diff_lq1, diff_lk1,
              diff_lq2, diff_lk2, diff_out_gain, gqa_q_gain, gqa_k_gain, w_o,
              ffn_norm, w_gate_up, w_down):
    B, S, _ = x.shape
    rows = S // GRID_W
    pos = jnp.arange(S, dtype=jnp.int32)
    cos_mla, sin_mla = _rope_cos_sin(pos, MLA_ROPE)
    cos_dif, sin_dif = _rope_cos_sin(pos, DIFF_QK)
    row_idx, col_idx = jnp.meshgrid(jnp.arange(rows, dtype=jnp.int32),
                                    jnp.arange(GRID_W, dtype=jnp.int32), indexing="ij")
    half_ax = GQA_DIM // 2
    cos_row, sin_row = _rope_cos_sin(row_idx.reshape(-1), half_ax)
    cos_col, sin_col = _rope_cos_sin(col_idx.reshape(-1), half_ax)

    def axial_rope(t):
        return jnp.concatenate([_apply_rope(t[..., :half_ax], cos_row, sin_row),
                                _apply_rope(t[..., half_ax:], cos_col, sin_col)], axis=-1)

    for l in range(DEPTH):
        xn = _rms_norm(x, attn_norm[l])
        h = xn @ w_in[l]
        (h_cq, h_ckv, h_kr, h_dq, h_dk, h_dv, h_gq, h_gk, h_gv) = jnp.split(h, IN_OFFSETS, axis=-1)

        c_q = _rms_norm(h_cq, mla_q_norm[l])
        q_a = (c_q @ mla_w_uq[l]).reshape(B, S, MLA_HEADS, MLA_QK).transpose(0, 2, 1, 3)
        c_kv = _rms_norm(h_ckv, mla_kv_norm[l])
        kv_a = (c_kv @ mla_w_ukv[l]).reshape(B, S, MLA_HEADS, MLA_NOPE + MLA_V).transpose(0, 2, 1, 3)
        k_nope, v_a = kv_a[..., :MLA_NOPE], kv_a[..., MLA_NOPE:]
        k_rope = jnp.broadcast_to(h_kr[:, None, :, :], (B, MLA_HEADS, S, MLA_ROPE))
        k_a = jnp.concatenate([k_nope, k_rope], axis=-1)
        q_a = _rms_norm(q_a, mla_q_gain[l])
        k_a = _rms_norm(k_a, mla_k_gain[l])
        q_a = jnp.concatenate([q_a[..., :MLA_NOPE], _apply_rope(q_a[..., MLA_NOPE:], cos_mla, sin_mla)], axis=-1)
        k_a = jnp.concatenate([k_a[..., :MLA_NOPE], _apply_rope(k_a[..., MLA_NOPE:], cos_mla, sin_mla)], axis=-1)
        o_a = _dense_attention(q_a, k_a, v_a, MLA_QK ** -0.5,
                               "bhqd,bhsd->bhqs", "bhqs,bhsv->bhqv")
        o_a = o_a.transpose(0, 2, 1, 3).reshape(B, S, MLA_HEADS * MLA_V)

        q_b = h_dq.reshape(B, S, DIFF_HEADS, 2, DIFF_QK).transpose(0, 2, 3, 1, 4)
        k_b = h_dk.reshape(B, S, DIFF_HEADS, 2, DIFF_QK).transpose(0, 2, 3, 1, 4)
        v_b = h_dv.reshape(B, S, DIFF_HEADS, DIFF_V).transpose(0, 2, 1, 3)
        q_b = _apply_rope(_rms_norm(q_b, diff_q_gain[l]), cos_dif, sin_dif)
        k_b = _apply_rope(_rms_norm(k_b, diff_k_gain[l]), cos_dif, sin_dif)
        lambda_init = 0.8 - 0.6 * math.exp(-0.3 * l)
        lam = (jnp.exp(jnp.sum(diff_lq1[l].astype(jnp.float32) * diff_lk1[l].astype(jnp.float32)))
               - jnp.exp(jnp.sum(diff_lq2[l].astype(jnp.float32) * diff_lk2[l].astype(jnp.float32)))
               + lambda_init)
        scale_b = DIFF_QK ** -0.5

        def diff_block(qb, k_b=k_b, v_b=v_b, lam=lam):
            s = jnp.einsum("bhcqd,bhcsd->bhcqs", qb, k_b).astype(jnp.float32) * scale_b
            p = jax.nn.softmax(s, axis=-1)
            a = p[:, :, 0] - lam * p[:, :, 1]
            return jnp.einsum("bhqs,bhsv->bhqv", a.astype(v_b.dtype), v_b)

        o_b = _sweep_query_blocks(diff_block, q_b)
        o_b = (_rms_norm(o_b, diff_out_gain[l]) * (1.0 - lambda_init)).astype(x.dtype)
        o_b = o_b.transpose(0, 2, 1, 3).reshape(B, S, DIFF_HEADS * DIFF_V)

        q_c = h_gq.reshape(B, S, GQA_KV_HEADS, GQA_GROUP, GQA_DIM).transpose(0, 2, 3, 1, 4)
        k_c = h_gk.reshape(B, S, GQA_KV_HEADS, GQA_DIM).transpose(0, 2, 1, 3)
        v_c = h_gv.reshape(B, S, GQA_KV_HEADS, GQA_DIM).transpose(0, 2, 1, 3)
        q_c = axial_rope(_rms_norm(q_c, gqa_q_gain[l]))
        k_c = axial_rope(_rms_norm(k_c, gqa_k_gain[l]))
        o_c = _dense_attention(q_c, k_c, v_c, GQA_DIM ** -0.5,
                               "bkgqd,bksd->bkgqs", "bkgqs,bksv->bkgqv")
        o_c = o_c.transpose(0, 3, 1, 2, 4).reshape(B, S, GQA_HEADS * GQA_DIM)

        mix = jnp.concatenate([o_a, o_b, o_c], axis=-1)
        x = x + mix @ w_o[l]

        xn = _rms_norm(x, ffn_norm[l])
        gate, up = jnp.split(xn @ w_gate_up[l], 2, axis=-1)
        x = x + (jax.nn.silu(gate) * up) @ w_down[l]
    return x
```

```python
import functools
import math

import jax
import jax.numpy as jnp
from jax import lax
from jax.experimental import pallas as pl
from jax.experimental.pallas import tpu as pltpu

D_MODEL = 1024
DEPTH = 2
MLA_HEADS = 6
MLA_Q_RANK = 384
MLA_KV_RANK = 256
MLA_NOPE = 64
MLA_ROPE = 32
MLA_QK = MLA_NOPE + MLA_ROPE
MLA_V = 64
DIFF_HEADS = 4
DIFF_QK = 64
DIFF_V = 128
GQA_HEADS = 6
GQA_KV_HEADS = 2
GQA_GROUP = GQA_HEADS // GQA_KV_HEADS
GQA_DIM = 64
GRID_W = 64
ROPE_THETA = 10000.0
EPS = 1e-6
FFN_HIDDEN = 2816

LANES = 128
HALF = 64
LOG2E = math.log2(math.e)

H_CQ, H_CKV, H_KR, H_DQ, H_DK, H_DV, H_GQ, H_GK, H_GV, H_END = (
    0, 384, 640, 768, 1280, 1792, 2304, 2688, 2816, 2944)
O_QA, O_KA, O_VA, O_QD, O_KD, O_VD, O_QG, O_KG, O_VG, O_END = (
    0, 768, 1536, 1920, 2432, 2944, 3456, 3840, 3968, 4096)

ROW_BLOCK = 256
Q_BLOCK = 256
VMEM_LIMIT = 56 * 1024 * 1024

F32 = jnp.float32
BF16 = jnp.bfloat16


def _const_spec(shape):
    nd = len(shape)
    return pl.BlockSpec(shape, lambda *_: (0,) * nd, pipeline_mode=pl.Buffered(1))


def _norm_rope(t, gain, seg, inv_d, cos, sin_a, sin_b, shift):
    ss = jnp.dot((t * t).astype(BF16), seg, preferred_element_type=F32)
    rinv = lax.rsqrt(ss * inv_d + EPS)
    u = t * gain
    r = (u * cos + pltpu.roll(u, shift, 1) * sin_a
         + pltpu.roll(u, LANES - shift, 1) * sin_b)
    return (r * rinv).astype(BF16)


def _row_rms(v, gain):
    ms = jnp.mean(v * v, axis=-1, keepdims=True)
    return v * lax.rsqrt(ms + EPS) * gain


def _prologue_kernel(x_ref, g1_ref, w1_ref, gqn_ref, wuq_ref, gkvn_ref, wuk_ref,
                     wuv_ref, lg_ref, rope_ref, seg_ref, out_ref):
    x = x_ref[0]
    xn = _row_rms(x, g1_ref[...]).astype(BF16)
    h = jnp.dot(xn, w1_ref[...], preferred_element_type=F32)

    seg96 = seg_ref[0]
    seg64 = seg_ref[1]
    mla_t = (rope_ref[0], rope_ref[1], rope_ref[2])
    dif_t = (rope_ref[3], rope_ref[4], rope_ref[5])
    gqa_t = (rope_ref[6], rope_ref[7], rope_ref[8])
    lg = lg_ref[...]

    cq = _row_rms(h[:, H_CQ:H_CKV], gqn_ref[...]).astype(BF16)
    qa = jnp.dot(cq, wuq_ref[...], preferred_element_type=F32)
    ckv = _row_rms(h[:, H_CKV:H_KR], gkvn_ref[...]).astype(BF16)
    kn = jnp.dot(ckv, wuk_ref[...], preferred_element_type=F32)
    va = jnp.dot(ckv, wuv_ref[...], preferred_element_type=F32)
    kr = h[:, H_KR:H_DQ]
    for hd in range(MLA_HEADS):
        c = hd * LANES
        out_ref[0, :, O_QA + c:O_QA + c + LANES] = _norm_rope(
            qa[:, c:c + LANES], lg[0:1], seg96, 1.0 / MLA_QK, *mla_t, MLA_ROPE // 2)
        out_ref[0, :, O_KA + c:O_KA + c + LANES] = _norm_rope(
            kn[:, c:c + LANES] + kr, lg[1:2], seg96, 1.0 / MLA_QK, *mla_t, MLA_ROPE // 2)
    out_ref[0, :, O_VA:O_QD] = va.astype(BF16)

    for j in range(DIFF_HEADS):
        c = j * LANES
        out_ref[0, :, O_QD + c:O_QD + c + LANES] = _norm_rope(
            h[:, H_DQ + c:H_DQ + c + LANES], lg[2:3], seg64, 1.0 / DIFF_QK, *dif_t, DIFF_QK // 2)
        out_ref[0, :, O_KD + c:O_KD + c + LANES] = _norm_rope(
            h[:, H_DK + c:H_DK + c + LANES], lg[3:4], seg64, 1.0 / DIFF_QK, *dif_t, DIFF_QK // 2)
    out_ref[0, :, O_VD:O_QG] = h[:, H_DV:H_GQ].astype(BF16)

    for j in range(GQA_GROUP):
        c = j * LANES
        out_ref[0, :, O_QG + c:O_QG + c + LANES] = _norm_rope(
            h[:, H_GQ + c:H_GQ + c + LANES], lg[4:5], seg64, 1.0 / GQA_DIM, *gqa_t, GQA_DIM // 4)
    out_ref[0, :, O_KG:O_VG] = _norm_rope(
        h[:, H_GK:H_GV], lg[5:6], seg64, 1.0 / GQA_DIM, *gqa_t, GQA_DIM // 4)
    out_ref[0, :, O_VG:O_END] = h[:, H_GV:H_END].astype(BF16)


def _prologue(x, g1, w1, gqn, wuq, gkvn, wuk, wuv, lane_gain, rope, seg):
    B, S, _ = x.shape
    nrb = S // ROW_BLOCK
    return pl.pallas_call(
        _prologue_kernel,
        grid=(nrb, B),
        in_specs=[
            pl.BlockSpec((1, ROW_BLOCK, D_MODEL), lambda i, b: (b, i, 0)),
            _const_spec(g1.shape), _const_spec(w1.shape), _const_spec(gqn.shape),
            _const_spec(wuq.shape), _const_spec(gkvn.shape), _const_spec(wuk.shape),
            _const_spec(wuv.shape), _const_spec(lane_gain.shape),
            pl.BlockSpec((9, ROW_BLOCK, LANES), lambda i, b: (0, i, 0)),
            _const_spec(seg.shape),
        ],
        out_specs=pl.BlockSpec((1, ROW_BLOCK, O_END), lambda i, b: (b, i, 0)),
        out_shape=jax.ShapeDtypeStruct((B, S, O_END), BF16),
        compiler_params=pltpu.CompilerParams(
            dimension_semantics=("arbitrary", "arbitrary"),
            vmem_limit_bytes=VMEM_LIMIT),
        name="prologue",
    )(x, g1, w1, gqn, wuq, gkvn, wuk, wuv, lane_gain, rope, seg)


def _scores_softmax(q, k):
    s = lax.dot_general(q, k, (((1,), (1,)), ((), ())), preferred_element_type=F32)
    m = jnp.max(s, axis=-1, keepdims=True)
    p = jnp.exp2(s - m)
    return p, jnp.sum(p, axis=-1, keepdims=True)


def _attend(q, k, v):
    p, l = _scores_softmax(q, k)
    o = jnp.dot(p.astype(BF16), v, preferred_element_type=F32)
    return o / l


def _split_halves(k_ref, k0_scr, k1_scr):
    k = k_ref[0]
    low = lax.broadcasted_iota(jnp.int32, k.shape, 1) < HALF
    zero = jnp.zeros_like(k)
    k0_scr[...] = jnp.where(low, k, zero)
    k1_scr[...] = jnp.where(low, zero, k)


def _mla_attn_kernel(q_ref, k_ref, v_ref, o_ref):
    seq = q_ref.shape[1]
    low = lax.broadcasted_iota(jnp.int32, (Q_BLOCK, LANES), 1) < HALF

    def body(i, carry):
        r0 = pl.multiple_of(i * Q_BLOCK, Q_BLOCK)
        o0 = _attend(q_ref[0, pl.ds(r0, Q_BLOCK), 0:LANES], k_ref[0, :, 0:LANES], v_ref[0])
        o1 = _attend(q_ref[0, pl.ds(r0, Q_BLOCK), LANES:2 * LANES],
                     k_ref[0, :, LANES:2 * LANES], v_ref[0])
        o_ref[0, pl.ds(r0, Q_BLOCK), :] = jnp.where(low, o0, o1).astype(BF16)
        return carry

    lax.fori_loop(0, seq // Q_BLOCK, body, 0)


def _gqa_attn_kernel(q_ref, k_ref, v_ref, o_ref, k0_scr, k1_scr):
    seq = q_ref.shape[1]
    _split_halves(k_ref, k0_scr, k1_scr)
    low = lax.broadcasted_iota(jnp.int32, (Q_BLOCK, LANES), 1) < HALF

    def body(i, carry):
        r0 = pl.multiple_of(i * Q_BLOCK, Q_BLOCK)
        q = q_ref[0, pl.ds(r0, Q_BLOCK), :]
        o0 = _attend(q, k0_scr[...], v_ref[0])
        o1 = _attend(q, k1_scr[...], v_ref[0])
        o_ref[0, pl.ds(r0, Q_BLOCK), :] = jnp.where(low, o0, o1).astype(BF16)
        return carry

    lax.fori_loop(0, seq // Q_BLOCK, body, 0)


def _diff_attn_kernel(lv_ref, og_ref, q_ref, k_ref, v_ref, o_ref, k0_scr, k1_scr, *,
                      lambda_init):
    seq = q_ref.shape[1]
    _split_halves(k_ref, k0_scr, k1_scr)
    lv = lv_ref[...]
    lam = (jnp.exp(jnp.sum(lv[0:1] * lv[1:2], axis=-1, keepdims=True))
           - jnp.exp(jnp.sum(lv[2:3] * lv[3:4], axis=-1, keepdims=True))
           + lambda_init)
    out_gain = og_ref[...] * (1.0 - lambda_init)

    def body(i, carry):
        r0 = pl.multiple_of(i * Q_BLOCK, Q_BLOCK)
        q = q_ref[0, pl.ds(r0, Q_BLOCK), :]
        p0, l0 = _scores_softmax(q, k0_scr[...])
        p1, l1 = _scores_softmax(q, k1_scr[...])
        a = p0 * (1.0 / l0) - p1 * (lam / l1)
        o = jnp.dot(a.astype(BF16), v_ref[0], preferred_element_type=F32)
        o_ref[0, pl.ds(r0, Q_BLOCK), :] = _row_rms(o, out_gain).astype(BF16)
        return carry

    lax.fori_loop(0, seq // Q_BLOCK, body, 0)


def _attn_params():
    return pltpu.CompilerParams(dimension_semantics=("arbitrary", "arbitrary"),
                                vmem_limit_bytes=VMEM_LIMIT)


def _mla_attention(qkv):
    B, S, _ = qkv.shape
    pairs = MLA_HEADS // 2
    return pl.pallas_call(
        _mla_attn_kernel,
        grid=(B, pairs),
        in_specs=[
            pl.BlockSpec((1, S, 2 * LANES), lambda b, p: (b, 0, O_QA // (2 * LANES) + p)),
            pl.BlockSpec((1, S, 2 * LANES), lambda b, p: (b, 0, O_KA // (2 * LANES) + p)),
            pl.BlockSpec((1, S, LANES), lambda b, p: (b, 0, O_VA // LANES + p)),
        ],
        out_specs=pl.BlockSpec((1, S, LANES), lambda b, p: (b, 0, p)),
        out_shape=jax.ShapeDtypeStruct((B, S, MLA_HEADS * MLA_V), BF16),
        compiler_params=_attn_params(),
        name="mla_attention",
    )(qkv, qkv, qkv)


def _diff_attention(qkv, lvec, out_gain, lambda_init):
    B, S, _ = qkv.shape
    return pl.pallas_call(
        functools.partial(_diff_attn_kernel, lambda_init=lambda_init),
        grid=(B, DIFF_HEADS),
        in_specs=[
            _const_spec(lvec.shape), _const_spec(out_gain.shape),
            pl.BlockSpec((1, S, LANES), lambda b, h: (b, 0, O_QD // LANES + h)),
            pl.BlockSpec((1, S, LANES), lambda b, h: (b, 0, O_KD // LANES + h)),
            pl.BlockSpec((1, S, LANES), lambda b, h: (b, 0, O_VD // LANES + h)),
        ],
        out_specs=pl.BlockSpec((1, S, LANES), lambda b, h: (b, 0, h)),
        out_shape=jax.ShapeDtypeStruct((B, S, DIFF_HEADS * DIFF_V), BF16),
        scratch_shapes=[pltpu.VMEM((S, LANES), BF16), pltpu.VMEM((S, LANES), BF16)],
        compiler_params=_attn_params(),
        name="diff_attention",
    )(lvec, out_gain, qkv, qkv, qkv)


def _gqa_attention(qkv):
    B, S, _ = qkv.shape
    return pl.pallas_call(
        _gqa_attn_kernel,
        grid=(B, GQA_GROUP),
        in_specs=[
            pl.BlockSpec((1, S, LANES), lambda b, g: (b, 0, O_QG // LANES + g)),
            pl.BlockSpec((1, S, LANES), lambda b, g: (b, 0, O_KG // LANES)),
            pl.BlockSpec((1, S, LANES), lambda b, g: (b, 0, O_VG // LANES)),
        ],
        out_specs=pl.BlockSpec((1, S, LANES), lambda b, g: (b, 0, g)),
        out_shape=jax.ShapeDtypeStruct((B, S, GQA_HEADS * GQA_DIM), BF16),
        scratch_shapes=[pltpu.VMEM((S, LANES), BF16), pltpu.VMEM((S, LANES), BF16)],
        compiler_params=_attn_params(),
        name="gqa_attention",
    )(qkv, qkv, qkv)


def _epilogue_kernel(x_ref, oa_ref, ob_ref, oc_ref, woa_ref, wob_ref, woc_ref, g2_ref,
                     wgu_ref, wd_ref, y_ref):
    x1 = (x_ref[0]
          + jnp.dot(oa_ref[0], woa_ref[...], preferred_element_type=F32)
          + jnp.dot(ob_ref[0], wob_ref[...], preferred_element_type=F32)
          + jnp.dot(oc_ref[0], woc_ref[...], preferred_element_type=F32))
    xn = _row_rms(x1, g2_ref[...]).astype(BF16)
    gate = jnp.dot(xn, wgu_ref[:, 0:FFN_HIDDEN], preferred_element_type=F32)
    up = jnp.dot(xn, wgu_ref[:, FFN_HIDDEN:2 * FFN_HIDDEN], preferred_element_type=F32)
    act = (gate * (1.0 / (1.0 + jnp.exp(-gate))) * up).astype(BF16)
    y_ref[0] = x1 + jnp.dot(act, wd_ref[...], preferred_element_type=F32)


def _epilogue(x, oa, ob, oc, woa, wob, woc, g2, wgu, wd):
    B, S, _ = x.shape
    nrb = S // ROW_BLOCK

    def rows(width):
        return pl.BlockSpec((1, ROW_BLOCK, width), lambda b, i: (b, i, 0))

    return pl.pallas_call(
        _epilogue_kernel,
        grid=(B, nrb),
        in_specs=[
            rows(D_MODEL), rows(oa.shape[-1]), rows(ob.shape[-1]), rows(oc.shape[-1]),
            _const_spec(woa.shape), _const_spec(wob.shape), _const_spec(woc.shape),
            _const_spec(g2.shape), _const_spec(wgu.shape), _const_spec(wd.shape),
        ],
        out_specs=rows(D_MODEL),
        out_shape=jax.ShapeDtypeStruct(x.shape, F32),
        compiler_params=pltpu.CompilerParams(
            dimension_semantics=("arbitrary", "arbitrary"),
            vmem_limit_bytes=VMEM_LIMIT),
        name="epilogue",
    )(x, oa, ob, oc, woa, wob, woc, g2, wgu, wd)


def _rope_angles(pos, dim):
    inv = 1.0 / (ROPE_THETA ** (jnp.arange(0, dim, 2, dtype=F32) / dim))
    ang = pos.astype(F32)[:, None] * inv[None, :]
    return jnp.cos(ang), jnp.sin(ang)


def _rope_tables(seq):
    pos = jnp.arange(seq, dtype=jnp.int32)
    cm, sm = _rope_angles(pos, MLA_ROPE)
    cd, sd = _rope_angles(pos, DIFF_QK)
    cr, sr = _rope_angles(pos // GRID_W, GQA_DIM // 2)
    cc, sc = _rope_angles(pos % GRID_W, GQA_DIM // 2)
    one = lambda w: jnp.ones((seq, w), F32)
    zero = lambda w: jnp.zeros((seq, w), F32)
    cat = lambda parts: jnp.concatenate(parts, axis=-1)
    mla = [cat([one(64), cm, cm, one(32)]),
           cat([zero(64), zero(16), sm, zero(32)]),
           cat([zero(64), -sm, zero(16), zero(32)])]
    dif = [cat([cd, cd] * 2),
           cat([zero(32), sd] * 2),
           cat([-sd, zero(32)] * 2)]
    gqa = [cat([cr, cr, cc, cc] * 2),
           cat([zero(16), sr, zero(16), sc] * 2),
           cat([-sr, zero(16), -sc, zero(16)] * 2)]
    return jnp.stack(mla + dif + gqa)


def _segment_ones():
    lane = jnp.arange(LANES)
    seg96 = (lane[:, None] < MLA_QK) & (lane[None, :] < LANES)
    seg64 = (lane[:, None] // HALF) == (lane[None, :] // HALF)
    return jnp.stack([seg96, seg64]).astype(BF16)


_GQA_HEAD_ORDER = (0, 3, 1, 4, 2, 5)


def kernel(x, attn_norm, w_in, mla_q_norm, mla_w_uq, mla_kv_norm, mla_w_ukv,
           mla_q_gain, mla_k_gain, diff_q_gain, diff_k_gain, diff_lq1, diff_lk1,
           diff_lq2, diff_lk2, diff_out_gain, gqa_q_gain, gqa_k_gain, w_o,
           ffn_norm, w_gate_up, w_down):
    B, S, _ = x.shape
    L = DEPTH
    rope = _rope_tables(S)
    seg = _segment_ones()
    order = jnp.array(_GQA_HEAD_ORDER)

    zc = lambda w: jnp.zeros((L, D_MODEL, w), F32)
    gq = w_in[:, :, 2208:2592].reshape(L, D_MODEL, GQA_HEADS, GQA_DIM)[:, :, order]
    w1 = jnp.concatenate(
        [w_in[:, :, 0:640], zc(64), w_in[:, :, 640:672], zc(32), w_in[:, :, 672:2208],
         gq.reshape(L, D_MODEL, GQA_HEADS * GQA_DIM), w_in[:, :, 2592:2848]],
        axis=-1).astype(BF16)
    wuq = jnp.pad(mla_w_uq.reshape(L, MLA_Q_RANK, MLA_HEADS, MLA_QK),
                  ((0, 0), (0, 0), (0, 0), (0, LANES - MLA_QK))
                  ).reshape(L, MLA_Q_RANK, MLA_HEADS * LANES).astype(BF16)
    ukv = mla_w_ukv.reshape(L, MLA_KV_RANK, MLA_HEADS, MLA_NOPE + MLA_V)
    wuk = jnp.pad(ukv[..., :MLA_NOPE], ((0, 0), (0, 0), (0, 0), (0, LANES - MLA_NOPE))
                  ).reshape(L, MLA_KV_RANK, MLA_HEADS * LANES).astype(BF16)
    wuv = ukv[..., MLA_NOPE:].reshape(L, MLA_KV_RANK, MLA_HEADS * MLA_V).astype(BF16)
    n_a, n_b = MLA_HEADS * MLA_V, DIFF_HEADS * DIFF_V
    woa = w_o[:, 0:n_a].astype(BF16)
    wob = w_o[:, n_a:n_a + n_b].astype(BF16)
    woc = w_o[:, n_a + n_b:].reshape(L, GQA_HEADS, GQA_DIM, D_MODEL)[:, order].reshape(
        L, GQA_HEADS * GQA_DIM, D_MODEL).astype(BF16)
    wgu = w_gate_up.astype(BF16)
    wd = w_down.astype(BF16)

    pad96 = lambda g: jnp.pad(g, ((0, 0), (0, LANES - MLA_QK)))
    twice = lambda g: jnp.concatenate([g, g], axis=-1)
    lane_gain = jnp.stack(
        [pad96(mla_q_gain) * (MLA_QK ** -0.5 * LOG2E), pad96(mla_k_gain),
         twice(diff_q_gain) * (DIFF_QK ** -0.5 * LOG2E), twice(diff_k_gain),
         twice(gqa_q_gain) * (GQA_DIM ** -0.5 * LOG2E), twice(gqa_k_gain),
         jnp.zeros((L, LANES), F32), jnp.zeros((L, LANES), F32)], axis=1)
    pad64 = lambda g: jnp.pad(g, ((0, 0), (0, LANES - DIFF_QK)))
    lvec = jnp.stack([pad64(diff_lq1), pad64(diff_lk1), pad64(diff_lq2), pad64(diff_lk2)]
                     + [jnp.zeros((L, LANES), F32)] * 4, axis=1)

    for l in range(L):
        qkv = _prologue(x, attn_norm[l][None], w1[l], mla_q_norm[l][None], wuq[l],
                        mla_kv_norm[l][None], wuk[l], wuv[l], lane_gain[l], rope, seg)
        oa = _mla_attention(qkv)
        ob = _diff_attention(qkv, lvec[l], diff_out_gain[l][None],
                             0.8 - 0.6 * math.exp(-0.3 * l))
        oc = _gqa_attention(qkv)
        x = _epilogue(x, oa, ob, oc, woa[l], wob[l], woc[l], ffn_norm[l][None],
                      wgu[l], wd[l])
    return x
```

```python
import functools
import math

import jax
import jax.numpy as jnp
from jax import lax
from jax.experimental import pallas as pl
from jax.experimental.pallas import tpu as pltpu

D_MODEL = 1024
DEPTH = 2
MLA_HEADS = 6
MLA_Q_RANK = 384
MLA_KV_RANK = 256
MLA_NOPE = 64
MLA_ROPE = 32
MLA_QK = MLA_NOPE + MLA_ROPE
MLA_V = 64
DIFF_HEADS = 4
DIFF_QK = 64
DIFF_V = 128
GQA_HEADS = 6
GQA_KV_HEADS = 2
GQA_GROUP = GQA_HEADS // GQA_KV_HEADS
GQA_DIM = 64
GRID_W = 64
ROPE_THETA = 10000.0
EPS = 1e-6
FFN_HIDDEN = 2816

LANES = 128
HALF = 64
LOG2E = math.log2(math.e)

H_CQ, H_CKV, H_KR, H_DQ, H_DK, H_DV, H_GQ, H_GK, H_GV, H_END = (
    0, 384, 640, 768, 1280, 1792, 2304, 2688, 2816, 2944)
O_QA, O_KA, O_VA, O_QD, O_KD, O_VD, O_QG, O_KG, O_VG, O_END = (
    0, 768, 1536, 1920, 2432, 2944, 3456, 3840, 3968, 4096)

ROW_BLOCK = 256
Q_BLOCK = 256
KEY_CHUNK = 256
VMEM_LIMIT = 56 * 1024 * 1024

F32 = jnp.float32
BF16 = jnp.bfloat16


def _const_spec(shape):
    nd = len(shape)
    return pl.BlockSpec(shape, lambda *_: (0,) * nd, pipeline_mode=pl.Buffered(1))


def _norm_rope(t, gain, seg, inv_d, cos, sin_a, sin_b, shift):
    ss = jnp.dot((t * t).astype(BF16), seg, preferred_element_type=F32)
    rinv = lax.rsqrt(ss * inv_d + EPS)
    u = t * gain
    r = (u * cos + pltpu.roll(u, shift, 1) * sin_a
         + pltpu.roll(u, LANES - shift, 1) * sin_b)
    return (r * rinv).astype(BF16)


def _row_rms(v, gain):
    ms = jnp.mean(v * v, axis=-1, keepdims=True)
    return v * lax.rsqrt(ms + EPS) * gain


def _prologue_kernel(x_ref, g1_ref, w1_ref, gqn_ref, wuq_ref, gkvn_ref, wuk_ref,
                     wuv_ref, lg_ref, rope_ref, seg_ref, out_ref):
    x = x_ref[0]
    xn = _row_rms(x, g1_ref[...]).astype(BF16)
    h = jnp.dot(xn, w1_ref[...], preferred_element_type=F32)

    seg96 = seg_ref[0]
    seg64 = seg_ref[1]
    mla_t = (rope_ref[0], rope_ref[1], rope_ref[2])
    dif_t = (rope_ref[3], rope_ref[4], rope_ref[5])
    gqa_t = (rope_ref[6], rope_ref[7], rope_ref[8])
    lg = lg_ref[...]

    cq = _row_rms(h[:, H_CQ:H_CKV], gqn_ref[...]).astype(BF16)
    qa = jnp.dot(cq, wuq_ref[...], preferred_element_type=F32)
    ckv = _row_rms(h[:, H_CKV:H_KR], gkvn_ref[...]).astype(BF16)
    kn = jnp.dot(ckv, wuk_ref[...], preferred_element_type=F32)
    va = jnp.dot(ckv, wuv_ref[...], preferred_element_type=F32)
    kr = h[:, H_KR:H_DQ]
    for hd in range(MLA_HEADS):
        c = hd * LANES
        out_ref[0, :, O_QA + c:O_QA + c + LANES] = _norm_rope(
            qa[:, c:c + LANES], lg[0:1], seg96, 1.0 / MLA_QK, *mla_t, MLA_ROPE // 2)
        out_ref[0, :, O_KA + c:O_KA + c + LANES] = _norm_rope(
            kn[:, c:c + LANES] + kr, lg[1:2], seg96, 1.0 / MLA_QK, *mla_t, MLA_ROPE // 2)
    out_ref[0, :, O_VA:O_QD] = va.astype(BF16)

    for j in range(DIFF_HEADS):
        c = j * LANES
        out_ref[0, :, O_QD + c:O_QD + c + LANES] = _norm_rope(
            h[:, H_DQ + c:H_DQ + c + LANES], lg[2:3], seg64, 1.0 / DIFF_QK, *dif_t, DIFF_QK // 2)
        out_ref[0, :, O_KD + c:O_KD + c + LANES] = _norm_rope(
            h[:, H_DK + c:H_DK + c + LANES], lg[3:4], seg64, 1.0 / DIFF_QK, *dif_t, DIFF_QK // 2)
    out_ref[0, :, O_VD:O_QG] = h[:, H_DV:H_GQ].astype(BF16)

    for j in range(GQA_GROUP):
        c = j * LANES
        out_ref[0, :, O_QG + c:O_QG + c + LANES] = _norm_rope(
            h[:, H_GQ + c:H_GQ + c + LANES], lg[4:5], seg64, 1.0 / GQA_DIM, *gqa_t, GQA_DIM // 4)
    out_ref[0, :, O_KG:O_VG] = _norm_rope(
        h[:, H_GK:H_GV], lg[5:6], seg64, 1.0 / GQA_DIM, *gqa_t, GQA_DIM // 4)
    out_ref[0, :, O_VG:O_END] = h[:, H_GV:H_END].astype(BF16)


def _prologue(x, g1, w1, gqn, wuq, gkvn, wuk, wuv, lane_gain, rope, seg):
    B, S, _ = x.shape
    nrb = S // ROW_BLOCK
    return pl.pallas_call(
        _prologue_kernel,
        grid=(nrb, B),
        in_specs=[
            pl.BlockSpec((1, ROW_BLOCK, D_MODEL), lambda i, b: (b, i, 0)),
            _const_spec(g1.shape), _const_spec(w1.shape), _const_spec(gqn.shape),
            _const_spec(wuq.shape), _const_spec(gkvn.shape), _const_spec(wuk.shape),
            _const_spec(wuv.shape), _const_spec(lane_gain.shape),
            pl.BlockSpec((9, ROW_BLOCK, LANES), lambda i, b: (0, i, 0)),
            _const_spec(seg.shape),
        ],
        out_specs=pl.BlockSpec((1, ROW_BLOCK, O_END), lambda i, b: (b, i, 0)),
        out_shape=jax.ShapeDtypeStruct((B, S, O_END), BF16),
        compiler_params=pltpu.CompilerParams(
            dimension_semantics=("arbitrary", "arbitrary"),
            vmem_limit_bytes=VMEM_LIMIT),
        name="prologue",
    )(x, g1, w1, gqn, wuq, gkvn, wuk, wuv, lane_gain, rope, seg)


def _lane_tiles(a):
    return [a[:, j * LANES:(j + 1) * LANES] for j in range(a.shape[1] // LANES)]


def _interleave(*gens):
    live = list(gens)
    while live:
        for g in list(live):
            if next(g, _DONE) is _DONE:
                live.remove(g)


_DONE = object()


def _attention_pipeline(q_ref, q_cols, v_ref, o_ref, scratch, combine):
    kt_scr, s_scr, m_scr, p_scr, v_scr = scratch
    seq = q_ref.shape[1]
    v_scr[:, 0:LANES] = v_ref[0]
    v_scr[:, LANES:2 * LANES] = jnp.ones((seq, LANES), BF16)
    nblk = seq // Q_BLOCK
    nchunk = seq // KEY_CHUNK

    def score_pass(r0, h):
        q = q_ref[0, pl.ds(r0, Q_BLOCK), q_cols[h]:q_cols[h] + LANES]
        mx = None
        for c in range(nchunk):
            ks = slice(c * KEY_CHUNK, (c + 1) * KEY_CHUNK)
            s = jnp.dot(q, kt_scr[h, :, ks], preferred_element_type=F32)
            s_scr[h, :, ks] = s
            part = functools.reduce(jnp.maximum, _lane_tiles(s))
            mx = part if mx is None else jnp.maximum(mx, part)
            yield
        m_scr[h] = jnp.broadcast_to(jnp.max(mx, axis=-1, keepdims=True), (Q_BLOCK, LANES))
        yield

    def value_pass(h, outs):
        mb = m_scr[h]
        for c in range(nchunk):
            ks = slice(c * KEY_CHUNK, (c + 1) * KEY_CHUNK)
            p = jnp.concatenate([jnp.exp2(t - mb) for t in _lane_tiles(s_scr[h, :, ks])],
                                axis=1)
            p_scr[h, :, ks] = p.astype(BF16)
            yield
        acc = jnp.dot(p_scr[h], v_scr[...], preferred_element_type=F32)
        outs.append(acc[:, 0:LANES] / acc[:, LANES:2 * LANES])
        yield

    def block(r0, r_next):
        outs = []
        _interleave(value_pass(0, outs), score_pass(r0, 1))
        if r_next is None:
            _interleave(value_pass(1, outs))
        else:
            _interleave(value_pass(1, outs), score_pass(r_next, 0))
        o_ref[0, pl.ds(r0, Q_BLOCK), :] = combine(outs[0], outs[1]).astype(BF16)

    _interleave(score_pass(0, 0))

    def body(i, carry):
        r0 = pl.multiple_of(i * Q_BLOCK, Q_BLOCK)
        block(r0, pl.multiple_of(r0 + Q_BLOCK, Q_BLOCK))
        return carry

    lax.fori_loop(0, nblk - 1, body, 0)
    block((nblk - 1) * Q_BLOCK, None)


def _pick_halves(o0, o1):
    low = lax.broadcasted_iota(jnp.int32, o0.shape, 1) < HALF
    return jnp.where(low, o0, o1)


def _split_halves_t(k_ref, kt_scr):
    kt = k_ref[0].T
    low = lax.broadcasted_iota(jnp.int32, kt.shape, 0) < HALF
    zero = jnp.zeros_like(kt)
    kt_scr[0] = jnp.where(low, kt, zero)
    kt_scr[1] = jnp.where(low, zero, kt)


def _mla_attn_kernel(q_ref, k_ref, v_ref, o_ref, *scratch):
    kt_scr = scratch[0]
    kt_scr[0] = k_ref[0, :, 0:LANES].T
    kt_scr[1] = k_ref[0, :, LANES:2 * LANES].T
    _attention_pipeline(q_ref, (0, LANES), v_ref, o_ref, scratch, _pick_halves)


def _gqa_attn_kernel(q_ref, k_ref, v_ref, o_ref, *scratch):
    _split_halves_t(k_ref, scratch[0])
    _attention_pipeline(q_ref, (0, 0), v_ref, o_ref, scratch, _pick_halves)


def _diff_attn_kernel(lv_ref, og_ref, q_ref, k_ref, v_ref, o_ref, *scratch, lambda_init):
    _split_halves_t(k_ref, scratch[0])
    lv = lv_ref[...]
    lam = (jnp.exp(jnp.sum(lv[0:1] * lv[1:2], axis=-1, keepdims=True))
           - jnp.exp(jnp.sum(lv[2:3] * lv[3:4], axis=-1, keepdims=True))
           + lambda_init)
    out_gain = og_ref[...] * (1.0 - lambda_init)

    def combine(o0, o1):
        return _row_rms(o0 - lam * o1, out_gain)

    _attention_pipeline(q_ref, (0, 0), v_ref, o_ref, scratch, combine)


def _attn_scratch(seq):
    return [pltpu.VMEM((2, LANES, seq), BF16),
            pltpu.VMEM((2, Q_BLOCK, seq), F32),
            pltpu.VMEM((2, Q_BLOCK, LANES), F32),
            pltpu.VMEM((2, Q_BLOCK, seq), BF16),
            pltpu.VMEM((seq, 2 * LANES), BF16)]


def _attn_params():
    return pltpu.CompilerParams(dimension_semantics=("arbitrary", "arbitrary"),
                                vmem_limit_bytes=VMEM_LIMIT)


def _mla_attention(qkv):
    B, S, _ = qkv.shape
    pairs = MLA_HEADS // 2
    return pl.pallas_call(
        _mla_attn_kernel,
        grid=(B, pairs),
        in_specs=[
            pl.BlockSpec((1, S, 2 * LANES), lambda b, p: (b, 0, O_QA // (2 * LANES) + p)),
            pl.BlockSpec((1, S, 2 * LANES), lambda b, p: (b, 0, O_KA // (2 * LANES) + p)),
            pl.BlockSpec((1, S, LANES), lambda b, p: (b, 0, O_VA // LANES + p)),
        ],
        out_specs=pl.BlockSpec((1, S, LANES), lambda b, p: (b, 0, p)),
        out_shape=jax.ShapeDtypeStruct((B, S, MLA_HEADS * MLA_V), BF16),
        scratch_shapes=_attn_scratch(S),
        compiler_params=_attn_params(),
        name="mla_attention",
    )(qkv, qkv, qkv)


def _diff_attention(qkv, lvec, out_gain, lambda_init):
    B, S, _ = qkv.shape
    return pl.pallas_call(
        functools.partial(_diff_attn_kernel, lambda_init=lambda_init),
        grid=(B, DIFF_HEADS),
        in_specs=[
            _const_spec(lvec.shape), _const_spec(out_gain.shape),
            pl.BlockSpec((1, S, LANES), lambda b, h: (b, 0, O_QD // LANES + h)),
            pl.BlockSpec((1, S, LANES), lambda b, h: (b, 0, O_KD // LANES + h)),
            pl.BlockSpec((1, S, LANES), lambda b, h: (b, 0, O_VD // LANES + h)),
        ],
        out_specs=pl.BlockSpec((1, S, LANES), lambda b, h: (b, 0, h)),
        out_shape=jax.ShapeDtypeStruct((B, S, DIFF_HEADS * DIFF_V), BF16),
        scratch_shapes=_attn_scratch(S),
        compiler_params=_attn_params(),
        name="diff_attention",
    )(lvec, out_gain, qkv, qkv, qkv)


def _gqa_attention(qkv):
    B, S, _ = qkv.shape
    return pl.pallas_call(
        _gqa_attn_kernel,
        grid=(B, GQA_GROUP),
        in_specs=[
            pl.BlockSpec((1, S, LANES), lambda b, g: (b, 0, O_QG // LANES + g)),
            pl.BlockSpec((1, S, LANES), lambda b, g: (b, 0, O_KG // LANES)),
            pl.BlockSpec((1, S, LANES), lambda b, g: (b, 0, O_VG // LANES)),
        ],
        out_specs=pl.BlockSpec((1, S, LANES), lambda b, g: (b, 0, g)),
        out_shape=jax.ShapeDtypeStruct((B, S, GQA_HEADS * GQA_DIM), BF16),
        scratch_shapes=_attn_scratch(S),
        compiler_params=_attn_params(),
        name="gqa_attention",
    )(qkv, qkv, qkv)


def _epilogue_kernel(x_ref, oa_ref, ob_ref, oc_ref, woa_ref, wob_ref, woc_ref, g2_ref,
                     wgu_ref, wd_ref, y_ref):
    x1 = (x_ref[0]
          + jnp.dot(oa_ref[0], woa_ref[...], preferred_element_type=F32)
          + jnp.dot(ob_ref[0], wob_ref[...], preferred_element_type=F32)
          + jnp.dot(oc_ref[0], woc_ref[...], preferred_element_type=F32))
    xn = _row_rms(x1, g2_ref[...]).astype(BF16)
    gate = jnp.dot(xn, wgu_ref[:, 0:FFN_HIDDEN], preferred_element_type=F32)
    up = jnp.dot(xn, wgu_ref[:, FFN_HIDDEN:2 * FFN_HIDDEN], preferred_element_type=F32)
    act = (gate * (1.0 / (1.0 + jnp.exp(-gate))) * up).astype(BF16)
    y_ref[0] = x1 + jnp.dot(act, wd_ref[...], preferred_element_type=F32)


def _epilogue(x, oa, ob, oc, woa, wob, woc, g2, wgu, wd):
    B, S, _ = x.shape
    nrb = S // ROW_BLOCK

    def rows(width):
        return pl.BlockSpec((1, ROW_BLOCK, width), lambda b, i: (b, i, 0))

    return pl.pallas_call(
        _epilogue_kernel,
        grid=(B, nrb),
        in_specs=[
            rows(D_MODEL), rows(oa.shape[-1]), rows(ob.shape[-1]), rows(oc.shape[-1]),
            _const_spec(woa.shape), _const_spec(wob.shape), _const_spec(woc.shape),
            _const_spec(g2.shape), _const_spec(wgu.shape), _const_spec(wd.shape),
        ],
        out_specs=rows(D_MODEL),
        out_shape=jax.ShapeDtypeStruct(x.shape, F32),
        compiler_params=pltpu.CompilerParams(
            dimension_semantics=("arbitrary", "arbitrary"),
            vmem_limit_bytes=VMEM_LIMIT),
        name="epilogue",
    )(x, oa, ob, oc, woa, wob, woc, g2, wgu, wd)


def _rope_angles(pos, dim):
    inv = 1.0 / (ROPE_THETA ** (jnp.arange(0, dim, 2, dtype=F32) / dim))
    ang = pos.astype(F32)[:, None] * inv[None, :]
    return jnp.cos(ang), jnp.sin(ang)


def _rope_tables(seq):
    pos = jnp.arange(seq, dtype=jnp.int32)
    cm, sm = _rope_angles(pos, MLA_ROPE)
    cd, sd = _rope_angles(pos, DIFF_QK)
    cr, sr = _rope_angles(pos // GRID_W, GQA_DIM // 2)
    cc, sc = _rope_angles(pos % GRID_W, GQA_DIM // 2)
    one = lambda w: jnp.ones((seq, w), F32)
    zero = lambda w: jnp.zeros((seq, w), F32)
    cat = lambda parts: jnp.concatenate(parts, axis=-1)
    mla = [cat([one(64), cm, cm, one(32)]),
           cat([zero(64), zero(16), sm, zero(32)]),
           cat([zero(64), -sm, zero(16), zero(32)])]
    dif = [cat([cd, cd] * 2),
           cat([zero(32), sd] * 2),
           cat([-sd, zero(32)] * 2)]
    gqa = [cat([cr, cr, cc, cc] * 2),
           cat([zero(16), sr, zero(16), sc] * 2),
           cat([-sr, zero(16), -sc, zero(16)] * 2)]
    return jnp.stack(mla + dif + gqa)


def _segment_ones():
    lane = jnp.arange(LANES)
    seg96 = (lane[:, None] < MLA_QK) & (lane[None, :] < LANES)
    seg64 = (lane[:, None] // HALF) == (lane[None, :] // HALF)
    return jnp.stack([seg96, seg64]).astype(BF16)


_GQA_HEAD_ORDER = (0, 3, 1, 4, 2, 5)


def kernel(x, attn_norm, w_in, mla_q_norm, mla_w_uq, mla_kv_norm, mla_w_ukv,
           mla_q_gain, mla_k_gain, diff_q_gain, diff_k_gain, diff_lq1, diff_lk1,
           diff_lq2, diff_lk2, diff_out_gain, gqa_q_gain, gqa_k_gain, w_o,
           ffn_norm, w_gate_up, w_down):
    B, S, _ = x.shape
    L = DEPTH
    rope = _rope_tables(S)
    seg = _segment_ones()
    order = jnp.array(_GQA_HEAD_ORDER)

    zc = lambda w: jnp.zeros((L, D_MODEL, w), F32)
    gq = w_in[:, :, 2208:2592].reshape(L, D_MODEL, GQA_HEADS, GQA_DIM)[:, :, order]
    w1 = jnp.concatenate(
        [w_in[:, :, 0:640], zc(64), w_in[:, :, 640:672], zc(32), w_in[:, :, 672:2208],
         gq.reshape(L, D_MODEL, GQA_HEADS * GQA_DIM), w_in[:, :, 2592:2848]],
        axis=-1).astype(BF16)
    wuq = jnp.pad(mla_w_uq.reshape(L, MLA_Q_RANK, MLA_HEADS, MLA_QK),
                  ((0, 0), (0, 0), (0, 0), (0, LANES - MLA_QK))
                  ).reshape(L, MLA_Q_RANK, MLA_HEADS * LANES).astype(BF16)
    ukv = mla_w_ukv.reshape(L, MLA_KV_RANK, MLA_HEADS, MLA_NOPE + MLA_V)
    wuk = jnp.pad(ukv[..., :MLA_NOPE], ((0, 0), (0, 0), (0, 0), (0, LANES - MLA_NOPE))
                  ).reshape(L, MLA_KV_RANK, MLA_HEADS * LANES).astype(BF16)
    wuv = ukv[..., MLA_NOPE:].reshape(L, MLA_KV_RANK, MLA_HEADS * MLA_V).astype(BF16)
    n_a, n_b = MLA_HEADS * MLA_V, DIFF_HEADS * DIFF_V
    woa = w_o[:, 0:n_a].astype(BF16)
    wob = w_o[:, n_a:n_a + n_b].astype(BF16)
    woc = w_o[:, n_a + n_b:].reshape(L, GQA_HEADS, GQA_DIM, D_MODEL)[:, order].reshape(
        L, GQA_HEADS * GQA_DIM, D_MODEL).astype(BF16)
    wgu = w_gate_up.astype(BF16)
    wd = w_down.astype(BF16)

    pad96 = lambda g: jnp.pad(g, ((0, 0), (0, LANES - MLA_QK)))
    twice = lambda g: jnp.concatenate([g, g], axis=-1)
    lane_gain = jnp.stack(
        [pad96(mla_q_gain) * (MLA_QK ** -0.5 * LOG2E), pad96(mla_k_gain),
         twice(diff_q_gain) * (DIFF_QK ** -0.5 * LOG2E), twice(diff_k_gain),
         twice(gqa_q_gain) * (GQA_DIM ** -0.5 * LOG2E), twice(gqa_k_gain),
         jnp.zeros((L, LANES), F32), jnp.zeros((L, LANES), F32)], axis=1)
    pad64 = lambda g: jnp.pad(g, ((0, 0), (0, LANES - DIFF_QK)))
    lvec = jnp.stack([pad64(diff_lq1), pad64(diff_lk1), pad64(diff_lq2), pad64(diff_lk2)]
                     + [jnp.zeros((L, LANES), F32)] * 4, axis=1)

    for l in range(L):
        qkv = _prologue(x, attn_norm[l][None], w1[l], mla_q_norm[l][None], wuq[l],
                        mla_kv_norm[l][None], wuk[l], wuv[l], lane_gain[l], rope, seg)
        oa = _mla_attention(qkv)
        ob = _diff_attention(qkv, lvec[l], diff_out_gain[l][None],
                             0.8 - 0.6 * math.exp(-0.3 * l))
        oc = _gqa_attention(qkv)
        x = _epilogue(x, oa, ob, oc, woa[l], wob[l], woc[l], ffn_norm[l][None],
                      wgu[l], wd[l])
    return x
```

```python
import functools
import math

import numpy as np
import jax
import jax.numpy as jnp
from jax import lax
from jax.experimental import pallas as pl
from jax.experimental.pallas import tpu as pltpu

D_MODEL = 1024
DEPTH = 2
MLA_HEADS = 6
MLA_Q_RANK = 384
MLA_KV_RANK = 256
MLA_NOPE = 64
MLA_ROPE = 32
MLA_QK = MLA_NOPE + MLA_ROPE
MLA_V = 64
DIFF_HEADS = 4
DIFF_QK = 64
DIFF_V = 128
GQA_HEADS = 6
GQA_KV_HEADS = 2
GQA_GROUP = GQA_HEADS // GQA_KV_HEADS
GQA_DIM = 64
GRID_W = 64
ROPE_THETA = 10000.0
EPS = 1e-6
FFN_HIDDEN = 2816

LANES = 128
SUBLANES = 8
ONES_ROWS = 16
HALF = 64
LOG2E = math.log2(math.e)

H_CQ, H_CKV, H_KR, H_DQ, H_DK, H_DV, H_GQ, H_GK, H_GV, H_END = (
    0, 384, 640, 768, 1280, 1792, 2304, 2688, 2816, 2944)
O_QA, O_KA, O_VA, O_QD, O_KD, O_VD, O_QG, O_KG, O_VG, O_END = (
    0, 768, 1536, 1920, 2432, 2944, 3456, 3840, 3968, 4096)

ROW_BLOCK = 512
SUB_ROWS = 256
Q_BLOCK = 256
KEY_CHUNK = 256
VMEM_LIMIT = 56 * 1024 * 1024

F32 = jnp.float32
BF16 = jnp.bfloat16


def _const_spec(shape):
    nd = len(shape)
    return pl.BlockSpec(shape, lambda *_: (0,) * nd, pipeline_mode=pl.Buffered(1))


def _layer_spec(arr, layer):
    nd = arr.ndim
    return pl.BlockSpec((None,) + arr.shape[1:], lambda *_: (layer,) + (0,) * (nd - 1),
                        pipeline_mode=pl.Buffered(1))


def _norm_rope_pair(ta, tb, gain_a, gain_b, seg2, inv_d, cos, sin_a, sin_b, shift):
    sq = jnp.concatenate([ta * ta, tb * tb], axis=1).astype(BF16)
    ss = jnp.dot(sq, seg2, preferred_element_type=F32)
    outs = []
    for t, gain, s in ((ta, gain_a, ss[:, 0:LANES]), (tb, gain_b, ss[:, LANES:2 * LANES])):
        rinv = lax.rsqrt(s * inv_d + EPS)
        u = t * gain
        r = (u * cos + pltpu.roll(u, shift, 1) * sin_a
             + pltpu.roll(u, LANES - shift, 1) * sin_b)
        outs.append((r * rinv).astype(BF16))
    return outs


def _row_rms(v, gain):
    ms = jnp.mean(v * v, axis=-1, keepdims=True)
    return v * lax.rsqrt(ms + EPS) * gain


def _prologue_kernel(x_ref, g1_ref, w1_ref, gqn_ref, wuq_ref, gkvn_ref, wuk_ref,
                     wuv_ref, lg_ref, rope_ref, seg_ref, out_ref):
    nsub = ROW_BLOCK // SUB_ROWS
    rows = [slice(r * SUB_ROWS, (r + 1) * SUB_ROWS) for r in range(nsub)]
    hs = {}

    def project(r):
        xn = _row_rms(x_ref[0, rows[r]], g1_ref[...]).astype(BF16)
        hs[r] = jnp.dot(xn, w1_ref[...], preferred_element_type=F32)
        yield

    def finish(r):
        return _prologue_finish(rows[r], hs.pop(r), gqn_ref, wuq_ref, gkvn_ref, wuk_ref,
                                wuv_ref, lg_ref, rope_ref, seg_ref, out_ref)

    _interleave(project(0))
    for r in range(nsub):
        _interleave(finish(r), *([project(r + 1)] if r + 1 < nsub else []))


def _prologue_finish(rs, h, gqn_ref, wuq_ref, gkvn_ref, wuk_ref, wuv_ref, lg_ref,
                     rope_ref, seg_ref, out_ref):
    seg96 = seg_ref[0]
    seg64 = seg_ref[1]
    mla_t = (rope_ref[0, rs], rope_ref[1, rs], rope_ref[2, rs])
    dif_t = (rope_ref[3, rs], rope_ref[4, rs], rope_ref[5, rs])
    gqa_t = (rope_ref[6, rs], rope_ref[7, rs], rope_ref[8, rs])
    lg = lg_ref[...]

    def tile(a, off, j):
        return a[:, off + j * LANES:off + (j + 1) * LANES]

    def emit(jobs, seg2, inv_d, tables, shift):
        for (ta, ga, ca), (tb, gb, cb) in zip(jobs[0::2], jobs[1::2]):
            ra, rb = _norm_rope_pair(ta, tb, ga, gb, seg2, inv_d, *tables, shift)
            out_ref[0, rs, ca:ca + LANES] = ra
            out_ref[0, rs, cb:cb + LANES] = rb
            yield

    cq = _row_rms(h[:, H_CQ:H_CKV], gqn_ref[...]).astype(BF16)
    qa = jnp.dot(cq, wuq_ref[...], preferred_element_type=F32)
    ckv = _row_rms(h[:, H_CKV:H_KR], gkvn_ref[...]).astype(BF16)
    kn = jnp.dot(ckv, wuk_ref[...], preferred_element_type=F32)
    va = jnp.dot(ckv, wuv_ref[...], preferred_element_type=F32)
    kr = h[:, H_KR:H_DQ]
    out_ref[0, rs, O_VA:O_QD] = va.astype(BF16)
    yield

    jobs = [(tile(h, H_DQ, j), lg[2:3], O_QD + j * LANES) for j in range(DIFF_HEADS)]
    jobs += [(tile(h, H_DK, j), lg[3:4], O_KD + j * LANES) for j in range(DIFF_HEADS)]
    yield from emit(jobs, seg64, 1.0 / DIFF_QK, dif_t, DIFF_QK // 2)
    out_ref[0, rs, O_VD:O_QG] = h[:, H_DV:H_GQ].astype(BF16)

    jobs = [(tile(h, H_GQ, j), lg[4:5], O_QG + j * LANES) for j in range(GQA_GROUP)]
    jobs += [(tile(h, H_GK, 0), lg[5:6], O_KG)]
    yield from emit(jobs, seg64, 1.0 / GQA_DIM, gqa_t, GQA_DIM // 4)
    out_ref[0, rs, O_VG:O_END] = h[:, H_GV:H_END].astype(BF16)

    jobs = [(tile(qa, 0, j), lg[0:1], O_QA + j * LANES) for j in range(MLA_HEADS)]
    jobs += [(tile(kn, 0, j) + kr, lg[1:2], O_KA + j * LANES) for j in range(MLA_HEADS)]
    yield from emit(jobs, seg96, 1.0 / MLA_QK, mla_t, MLA_ROPE // 2)


def _prologue(layer, x, g1, w1, gqn, wuq, gkvn, wuk, wuv, lane_gain, rope, seg):
    B, S, _ = x.shape
    nrb = S // ROW_BLOCK
    params = (g1, w1, gqn, wuq, gkvn, wuk, wuv, lane_gain)
    return pl.pallas_call(
        _prologue_kernel,
        grid=(nrb, B),
        in_specs=[pl.BlockSpec((1, ROW_BLOCK, D_MODEL), lambda i, b: (b, i, 0))]
        + [_layer_spec(p, layer) for p in params]
        + [pl.BlockSpec((9, ROW_BLOCK, LANES), lambda i, b: (0, i, 0)),
           _const_spec(seg.shape)],
        out_specs=pl.BlockSpec((1, ROW_BLOCK, O_END), lambda i, b: (b, i, 0)),
        out_shape=jax.ShapeDtypeStruct((B, S, O_END), BF16),
        compiler_params=pltpu.CompilerParams(
            dimension_semantics=("arbitrary", "arbitrary"),
            vmem_limit_bytes=VMEM_LIMIT),
        name="prologue",
    )(x, g1, w1, gqn, wuq, gkvn, wuk, wuv, lane_gain, rope, seg)


def _lane_tiles(a):
    return [a[:, j * LANES:(j + 1) * LANES] for j in range(a.shape[1] // LANES)]


def _interleave(*gens):
    live = list(gens)
    while live:
        for g in list(live):
            if next(g, _DONE) is _DONE:
                live.remove(g)


_DONE = object()


def _attention_pipeline(q_ref, q_cols, keys, vt_scr, vt_slot, o_ref, scratch, combine):
    s_scr, m_scr, p_scr = scratch
    seq = q_ref.shape[1]
    nblk = seq // Q_BLOCK
    nchunk = seq // KEY_CHUNK

    def score_pass(r0, h):
        qt = q_ref[0, pl.ds(r0, Q_BLOCK), q_cols[h]:q_cols[h] + LANES].T
        mx = None
        for c in range(nchunk):
            ks = slice(c * KEY_CHUNK, (c + 1) * KEY_CHUNK)
            s = jnp.dot(keys(h, ks), qt, preferred_element_type=F32)
            s_scr[h, ks, :] = s
            part = jnp.max(s.reshape(KEY_CHUNK // SUBLANES, SUBLANES, Q_BLOCK), axis=0)
            mx = part if mx is None else jnp.maximum(mx, part)
            yield
        m_scr[h] = jnp.broadcast_to(jnp.max(mx, axis=0, keepdims=True), (SUBLANES, Q_BLOCK))
        yield

    def value_pass(h, outs):
        m = m_scr[h][None]
        for c in range(nchunk):
            ks = slice(c * KEY_CHUNK, (c + 1) * KEY_CHUNK)
            s = s_scr[h, ks, :].reshape(KEY_CHUNK // SUBLANES, SUBLANES, Q_BLOCK)
            p_scr[h, ks, :] = jnp.exp2(s - m).reshape(KEY_CHUNK, Q_BLOCK).astype(BF16)
            yield
        outs.append(jnp.dot(vt_scr[vt_slot[h]], p_scr[h], preferred_element_type=F32))
        yield

    def block(r0, r_next):
        outs = []
        _interleave(value_pass(0, outs), score_pass(r0, 1))
        if r_next is None:
            _interleave(value_pass(1, outs))
        else:
            _interleave(value_pass(1, outs), score_pass(r_next, 0))
        o_ref[0, pl.ds(r0, Q_BLOCK), :] = combine(outs[0], outs[1]).astype(BF16)

    _interleave(score_pass(0, 0))
    for i in range(nblk):
        block(i * Q_BLOCK, (i + 1) * Q_BLOCK if i + 1 < nblk else None)


def _normalise_t(acc, rows):
    return acc[0:rows] / acc[rows:rows + 1]


def _join_heads(acc0, acc1):
    return jnp.concatenate([_normalise_t(acc0, HALF), _normalise_t(acc1, HALF)], axis=0).T


def _head_values_t(v_ref, vt_scr):
    vt = v_ref[0].T
    for h in range(2):
        vt_scr[h, 0:HALF, :] = vt[h * HALF:(h + 1) * HALF]
        vt_scr[h, HALF:HALF + ONES_ROWS, :] = jnp.ones((ONES_ROWS, vt.shape[1]), BF16)


def _split_halves(k_ref, k_scr):
    k = k_ref[0]
    low = lax.broadcasted_iota(jnp.int32, k.shape, 1) < HALF
    zero = jnp.zeros_like(k)
    k_scr[0] = jnp.where(low, k, zero)
    k_scr[1] = jnp.where(low, zero, k)


def _mla_attn_kernel(q_ref, k_ref, v_ref, o_ref, vt_scr, k_scr, *scratch):
    _head_values_t(v_ref, vt_scr)
    k_scr[0] = k_ref[0, :, 0:LANES]
    k_scr[1] = k_ref[0, :, LANES:2 * LANES]
    keys = lambda h, ks: k_scr[h, ks, :]
    _attention_pipeline(q_ref, (0, LANES), keys, vt_scr, (0, 1), o_ref, scratch, _join_heads)


def _gqa_attn_kernel(q_ref, k_ref, v_ref, o_ref, vt_scr, k_scr, *scratch):
    _head_values_t(v_ref, vt_scr)
    _split_halves(k_ref, k_scr)
    keys = lambda h, ks: k_scr[h, ks, :]
    _attention_pipeline(q_ref, (0, 0), keys, vt_scr, (0, 1), o_ref, scratch, _join_heads)


def _diff_attn_kernel(lv_ref, og_ref, q_ref, k_ref, v_ref, o_ref, vt_scr, k_scr, *scratch,
                      lambda_init):
    vt_scr[0, 0:LANES, :] = v_ref[0].T
    vt_scr[0, LANES:LANES + ONES_ROWS, :] = jnp.ones((ONES_ROWS, v_ref.shape[1]), BF16)
    _split_halves(k_ref, k_scr)
    keys = lambda h, ks: k_scr[h, ks, :]
    lv = lv_ref[...]
    lam = (jnp.exp(jnp.sum(lv[0:1] * lv[1:2], axis=-1, keepdims=True))
           - jnp.exp(jnp.sum(lv[2:3] * lv[3:4], axis=-1, keepdims=True))
           + lambda_init)
    out_gain = og_ref[...] * (1.0 - lambda_init)

    def combine(acc0, acc1):
        ot = _normalise_t(acc0, LANES) - lam * _normalise_t(acc1, LANES)
        return _row_rms(ot.T, out_gain)

    _attention_pipeline(q_ref, (0, 0), keys, vt_scr, (0, 0), o_ref, scratch, combine)


def _attn_scratch(seq, value_slots, value_rows):
    return [
        pltpu.VMEM((value_slots, value_rows + ONES_ROWS, seq), BF16),
        pltpu.VMEM((2, seq, LANES), BF16),
        pltpu.VMEM((2, seq, Q_BLOCK), F32),
        pltpu.VMEM((2, SUBLANES, Q_BLOCK), F32),
        pltpu.VMEM((2, seq, Q_BLOCK), BF16)]


def _attn_params():
    return pltpu.CompilerParams(dimension_semantics=("arbitrary", "arbitrary"),
                                vmem_limit_bytes=VMEM_LIMIT)


def _mla_attention(qkv):
    B, S, _ = qkv.shape
    pairs = MLA_HEADS // 2
    return pl.pallas_call(
        _mla_attn_kernel,
        grid=(B, pairs),
        in_specs=[
            pl.BlockSpec((1, S, 2 * LANES), lambda b, p: (b, 0, O_QA // (2 * LANES) + p)),
            pl.BlockSpec((1, S, 2 * LANES), lambda b, p: (b, 0, O_KA // (2 * LANES) + p)),
            pl.BlockSpec((1, S, LANES), lambda b, p: (b, 0, O_VA // LANES + p)),
        ],
        out_specs=pl.BlockSpec((1, S, LANES), lambda b, p: (b, 0, p)),
        out_shape=jax.ShapeDtypeStruct((B, S, MLA_HEADS * MLA_V), BF16),
        scratch_shapes=_attn_scratch(S, 2, HALF),
        compiler_params=_attn_params(),
        name="mla_attention",
    )(qkv, qkv, qkv)


def _diff_attention(layer, qkv, lvec, out_gain):
    B, S, _ = qkv.shape
    lambda_init = 0.8 - 0.6 * math.exp(-0.3 * layer)
    return pl.pallas_call(
        functools.partial(_diff_attn_kernel, lambda_init=lambda_init),
        grid=(B, DIFF_HEADS),
        in_specs=[
            _layer_spec(lvec, layer), _layer_spec(out_gain, layer),
            pl.BlockSpec((1, S, LANES), lambda b, h: (b, 0, O_QD // LANES + h)),
            pl.BlockSpec((1, S, LANES), lambda b, h: (b, 0, O_KD // LANES + h)),
            pl.BlockSpec((1, S, LANES), lambda b, h: (b, 0, O_VD // LANES + h)),
        ],
        out_specs=pl.BlockSpec((1, S, LANES), lambda b, h: (b, 0, h)),
        out_shape=jax.ShapeDtypeStruct((B, S, DIFF_HEADS * DIFF_V), BF16),
        scratch_shapes=_attn_scratch(S, 1, LANES),
        compiler_params=_attn_params(),
        name="diff_attention",
    )(lvec, out_gain, qkv, qkv, qkv)


def _gqa_attention(qkv):
    B, S, _ = qkv.shape
    return pl.pallas_call(
        _gqa_attn_kernel,
        grid=(B, GQA_GROUP),
        in_specs=[
            pl.BlockSpec((1, S, LANES), lambda b, g: (b, 0, O_QG // LANES + g)),
            pl.BlockSpec((1, S, LANES), lambda b, g: (b, 0, O_KG // LANES)),
            pl.BlockSpec((1, S, LANES), lambda b, g: (b, 0, O_VG // LANES)),
        ],
        out_specs=pl.BlockSpec((1, S, LANES), lambda b, g: (b, 0, g)),
        out_shape=jax.ShapeDtypeStruct((B, S, GQA_HEADS * GQA_DIM), BF16),
        scratch_shapes=_attn_scratch(S, 2, HALF),
        compiler_params=_attn_params(),
        name="gqa_attention",
    )(qkv, qkv, qkv)


def _epilogue_kernel(x_ref, oa_ref, ob_ref, oc_ref, wo_ref, g2_ref, wgu_ref, wd_ref, y_ref):
    for r in range(ROW_BLOCK // SUB_ROWS):
        rs = slice(r * SUB_ROWS, (r + 1) * SUB_ROWS)
        mix = jnp.concatenate([oa_ref[0, rs], ob_ref[0, rs], oc_ref[0, rs]], axis=1)
        x1 = x_ref[0, rs] + jnp.dot(mix, wo_ref[...], preferred_element_type=F32)
        xn = _row_rms(x1, g2_ref[...]).astype(BF16)
        gate = jnp.dot(xn, wgu_ref[:, 0:FFN_HIDDEN], preferred_element_type=F32)
        up = jnp.dot(xn, wgu_ref[:, FFN_HIDDEN:2 * FFN_HIDDEN], preferred_element_type=F32)
        act = (gate * (1.0 / (1.0 + jnp.exp(-gate))) * up).astype(BF16)
        y_ref[0, rs] = x1 + jnp.dot(act, wd_ref[...], preferred_element_type=F32)


def _epilogue(layer, x, oa, ob, oc, wo, g2, wgu, wd):
    B, S, _ = x.shape
    nrb = S // ROW_BLOCK

    def rows(width):
        return pl.BlockSpec((1, ROW_BLOCK, width), lambda b, i: (b, i, 0))

    return pl.pallas_call(
        _epilogue_kernel,
        grid=(B, nrb),
        in_specs=[rows(D_MODEL), rows(oa.shape[-1]), rows(ob.shape[-1]), rows(oc.shape[-1])]
        + [_layer_spec(p, layer) for p in (wo, g2, wgu, wd)],
        out_specs=rows(D_MODEL),
        out_shape=jax.ShapeDtypeStruct(x.shape, F32),
        compiler_params=pltpu.CompilerParams(
            dimension_semantics=("arbitrary", "arbitrary"),
            vmem_limit_bytes=VMEM_LIMIT),
        name="epilogue",
    )(x, oa, ob, oc, wo, g2, wgu, wd)


def _rope_angles(pos, dim):
    inv = (1.0 / (ROPE_THETA ** (np.arange(0, dim, 2, dtype=np.float32) / dim))
           ).astype(np.float32)
    ang = pos.astype(np.float32)[:, None] * inv[None, :]
    return np.cos(ang), np.sin(ang)


def _rope_tables(seq):
    pos = np.arange(seq, dtype=np.int32)
    cm, sm = _rope_angles(pos, MLA_ROPE)
    cd, sd = _rope_angles(pos, DIFF_QK)
    cr, sr = _rope_angles(pos // GRID_W, GQA_DIM // 2)
    cc, sc = _rope_angles(pos % GRID_W, GQA_DIM // 2)
    one = lambda w: np.ones((seq, w), np.float32)
    zero = lambda w: np.zeros((seq, w), np.float32)
    cat = lambda parts: np.concatenate(parts, axis=-1)
    mla = [cat([one(64), cm, cm, one(32)]),
           cat([zero(64), zero(16), sm, zero(32)]),
           cat([zero(64), -sm, zero(16), zero(32)])]
    dif = [cat([cd, cd] * 2),
           cat([zero(32), sd] * 2),
           cat([-sd, zero(32)] * 2)]
    gqa = [cat([cr, cr, cc, cc] * 2),
           cat([zero(16), sr, zero(16), sc] * 2),
           cat([-sr, zero(16), -sc, zero(16)] * 2)]
    return jnp.asarray(np.stack(mla + dif + gqa).astype(np.float32))


def _segment_ones():
    lane = np.arange(2 * LANES)
    same_tile = (lane[:, None] // LANES) == (lane[None, :] // LANES)
    seg96 = same_tile & ((lane[:, None] % LANES) < MLA_QK)
    seg64 = (lane[:, None] // HALF) == (lane[None, :] // HALF)
    return jnp.asarray(np.stack([seg96, seg64]).astype(np.float32)).astype(BF16)


_GQA_HEAD_ORDER = (0, 3, 1, 4, 2, 5)


def kernel(x, attn_norm, w_in, mla_q_norm, mla_w_uq, mla_kv_norm, mla_w_ukv,
           mla_q_gain, mla_k_gain, diff_q_gain, diff_k_gain, diff_lq1, diff_lk1,
           diff_lq2, diff_lk2, diff_out_gain, gqa_q_gain, gqa_k_gain, w_o,
           ffn_norm, w_gate_up, w_down):
    B, S, _ = x.shape
    L = DEPTH
    rope = _rope_tables(S)
    seg = _segment_ones()
    order = jnp.array(_GQA_HEAD_ORDER)

    zc = lambda w: jnp.zeros((L, D_MODEL, w), F32)
    gq = w_in[:, :, 2208:2592].reshape(L, D_MODEL, GQA_HEADS, GQA_DIM)[:, :, order]
    w1 = jnp.concatenate(
        [w_in[:, :, 0:640], zc(64), w_in[:, :, 640:672], zc(32), w_in[:, :, 672:2208],
         gq.reshape(L, D_MODEL, GQA_HEADS * GQA_DIM), w_in[:, :, 2592:2848]],
        axis=-1).astype(BF16)
    wuq = jnp.pad(mla_w_uq.reshape(L, MLA_Q_RANK, MLA_HEADS, MLA_QK),
                  ((0, 0), (0, 0), (0, 0), (0, LANES - MLA_QK))
                  ).reshape(L, MLA_Q_RANK, MLA_HEADS * LANES).astype(BF16)
    ukv = mla_w_ukv.reshape(L, MLA_KV_RANK, MLA_HEADS, MLA_NOPE + MLA_V)
    wuk = jnp.pad(ukv[..., :MLA_NOPE], ((0, 0), (0, 0), (0, 0), (0, LANES - MLA_NOPE))
                  ).reshape(L, MLA_KV_RANK, MLA_HEADS * LANES).astype(BF16)
    wuv = ukv[..., MLA_NOPE:].reshape(L, MLA_KV_RANK, MLA_HEADS * MLA_V).astype(BF16)
    n_ab = MLA_HEADS * MLA_V + DIFF_HEADS * DIFF_V
    woc = w_o[:, n_ab:].reshape(L, GQA_HEADS, GQA_DIM, D_MODEL)[:, order]
    wo = jnp.concatenate([w_o[:, 0:n_ab], woc.reshape(L, GQA_HEADS * GQA_DIM, D_MODEL)],
                         axis=1).astype(BF16)
    wgu = w_gate_up.astype(BF16)
    wd = w_down.astype(BF16)

    pad96 = lambda g: jnp.pad(g, ((0, 0), (0, LANES - MLA_QK)))
    twice = lambda g: jnp.concatenate([g, g], axis=-1)
    lane_gain = jnp.stack(
        [pad96(mla_q_gain) * (MLA_QK ** -0.5 * LOG2E), pad96(mla_k_gain),
         twice(diff_q_gain) * (DIFF_QK ** -0.5 * LOG2E), twice(diff_k_gain),
         twice(gqa_q_gain) * (GQA_DIM ** -0.5 * LOG2E), twice(gqa_k_gain),
         jnp.zeros((L, LANES), F32), jnp.zeros((L, LANES), F32)], axis=1)
    pad64 = lambda g: jnp.pad(g, ((0, 0), (0, LANES - DIFF_QK)))
    lvec = jnp.stack([pad64(diff_lq1), pad64(diff_lk1), pad64(diff_lq2), pad64(diff_lk2)]
                     + [jnp.zeros((L, LANES), F32)] * 4, axis=1)

    row = lambda g: g[:, None, :]
    for l in range(L):
        qkv = _prologue(l, x, row(attn_norm), w1, row(mla_q_norm), wuq, row(mla_kv_norm),
                        wuk, wuv, lane_gain, rope, seg)
        oa = _mla_attention(qkv)
        ob = _diff_attention(l, qkv, lvec, row(diff_out_gain))
        oc = _gqa_attention(qkv)
        x = _epilogue(l, x, oa, ob, oc, wo, row(ffn_norm), wgu, wd)
    return x
```

```python
import functools
import math

import numpy as np
import jax
import jax.numpy as jnp
from jax import lax
from jax.experimental import pallas as pl
from jax.experimental.pallas import tpu as pltpu

D_MODEL = 1024
DEPTH = 2
MLA_HEADS = 6
MLA_Q_RANK = 384
MLA_KV_RANK = 256
MLA_NOPE = 64
MLA_ROPE = 32
MLA_QK = MLA_NOPE + MLA_ROPE
MLA_V = 64
DIFF_HEADS = 4
DIFF_QK = 64
DIFF_V = 128
GQA_HEADS = 6
GQA_KV_HEADS = 2
GQA_GROUP = GQA_HEADS // GQA_KV_HEADS
GQA_DIM = 64
GRID_W = 64
ROPE_THETA = 10000.0
EPS = 1e-6
FFN_HIDDEN = 2816

LANES = 128
HALF = 64
LOG2E = math.log2(math.e)

H_CQ, H_CKV, H_KR, H_DQ, H_DK, H_DV, H_GQ, H_GK, H_GV, H_END = (
    0, 384, 640, 768, 1280, 1792, 2304, 2688, 2816, 2944)
O_QA, O_KA, O_QD, O_KD, O_VD, O_VA, O_QG, O_KG, O_VG, O_END = (
    0, 768, 1536, 2048, 2560, 3072, 3456, 3840, 3968, 4096)

ROW_BLOCK = 512
SUB_ROWS = 256
Q_BLOCK = 256
GROUPS_PER_STEP = 1
KEY_CHUNK = 256
VMEM_LIMIT = 56 * 1024 * 1024

F32 = jnp.float32
BF16 = jnp.bfloat16


def _const_spec(shape):
    nd = len(shape)
    return pl.BlockSpec(shape, lambda *_: (0,) * nd, pipeline_mode=pl.Buffered(1))


def _layer_spec(arr, layer):
    nd = arr.ndim
    return pl.BlockSpec((None,) + arr.shape[1:], lambda *_: (layer,) + (0,) * (nd - 1),
                        pipeline_mode=pl.Buffered(1))


def _interleave(*gens):
    live = list(gens)
    while live:
        for g in list(live):
            if next(g, _DONE) is _DONE:
                live.remove(g)


_DONE = object()


def _norm_rope_pair(ta, tb, gain_a, gain_b, seg2, inv_d, cos, sin_a, sin_b, shift):
    sq = jnp.concatenate([ta * ta, tb * tb], axis=1).astype(BF16)
    ss = jnp.dot(sq, seg2, preferred_element_type=F32)
    outs = []
    for t, gain, s in ((ta, gain_a, ss[:, 0:LANES]), (tb, gain_b, ss[:, LANES:2 * LANES])):
        rinv = lax.rsqrt(s * inv_d + EPS)
        u = t * gain
        r = (u * cos + pltpu.roll(u, shift, 1) * sin_a
             + pltpu.roll(u, LANES - shift, 1) * sin_b)
        outs.append((r * rinv).astype(BF16))
    return outs


def _row_rms(v, gain):
    ms = jnp.mean(v * v, axis=-1, keepdims=True)
    return v * lax.rsqrt(ms + EPS) * gain


def _prologue_kernel(x_ref, g1_ref, w1_ref, gqn_ref, wuq_ref, gkvn_ref, wuk_ref,
                     wuv_ref, lg_ref, rope_ref, seg_ref, out_ref):
    nsub = ROW_BLOCK // SUB_ROWS
    rows = [slice(r * SUB_ROWS, (r + 1) * SUB_ROWS) for r in range(nsub)]
    hs = {}

    def project(r):
        xn = _row_rms(x_ref[0, rows[r]], g1_ref[...]).astype(BF16)
        hs[r] = jnp.dot(xn, w1_ref[...], preferred_element_type=F32)
        yield

    def finish(r):
        return _prologue_finish(rows[r], hs.pop(r), gqn_ref, wuq_ref, gkvn_ref, wuk_ref,
                                wuv_ref, lg_ref, rope_ref, seg_ref, out_ref)

    _interleave(project(0))
    for r in range(nsub):
        _interleave(finish(r), *([project(r + 1)] if r + 1 < nsub else []))


def _prologue_finish(rs, h, gqn_ref, wuq_ref, gkvn_ref, wuk_ref, wuv_ref, lg_ref,
                     rope_ref, seg_ref, out_ref):
    seg96 = seg_ref[0]
    seg64 = seg_ref[1]
    mla_t = (rope_ref[0, rs], rope_ref[1, rs], rope_ref[2, rs])
    dif_t = (rope_ref[3, rs], rope_ref[4, rs], rope_ref[5, rs])
    gqa_t = (rope_ref[6, rs], rope_ref[7, rs], rope_ref[8, rs])
    lg = lg_ref[...]

    def tile(a, off, j):
        return a[:, off + j * LANES:off + (j + 1) * LANES]

    def emit(jobs, seg2, inv_d, tables, shift):
        for (ta, ga, ca), (tb, gb, cb) in zip(jobs[0::2], jobs[1::2]):
            ra, rb = _norm_rope_pair(ta, tb, ga, gb, seg2, inv_d, *tables, shift)
            out_ref[0, rs, ca:ca + LANES] = ra
            out_ref[0, rs, cb:cb + LANES] = rb
            yield

    cq = _row_rms(h[:, H_CQ:H_CKV], gqn_ref[...]).astype(BF16)
    qa = jnp.dot(cq, wuq_ref[...], preferred_element_type=F32)
    ckv = _row_rms(h[:, H_CKV:H_KR], gkvn_ref[...]).astype(BF16)
    kn = jnp.dot(ckv, wuk_ref[...], preferred_element_type=F32)
    va = jnp.dot(ckv, wuv_ref[...], preferred_element_type=F32)
    kr = h[:, H_KR:H_DQ]
    out_ref[0, rs, O_VA:O_VA + MLA_HEADS * MLA_V] = va.astype(BF16)
    yield

    jobs = [(tile(h, H_DQ, j), lg[2:3], O_QD + j * LANES) for j in range(DIFF_HEADS)]
    jobs += [(tile(h, H_DK, j), lg[3:4], O_KD + j * LANES) for j in range(DIFF_HEADS)]
    yield from emit(jobs, seg64, 1.0 / DIFF_QK, dif_t, DIFF_QK // 2)
    out_ref[0, rs, O_VD:O_VD + DIFF_HEADS * DIFF_V] = h[:, H_DV:H_GQ].astype(BF16)

    jobs = [(tile(h, H_GQ, j), lg[4:5], O_QG + j * LANES) for j in range(GQA_GROUP)]
    jobs += [(tile(h, H_GK, 0), lg[5:6], O_KG)]
    yield from emit(jobs, seg64, 1.0 / GQA_DIM, gqa_t, GQA_DIM // 4)
    out_ref[0, rs, O_VG:O_END] = h[:, H_GV:H_END].astype(BF16)

    jobs = [(tile(qa, 0, j), lg[0:1], O_QA + j * LANES) for j in range(MLA_HEADS)]
    jobs += [(tile(kn, 0, j) + kr, lg[1:2], O_KA + j * LANES) for j in range(MLA_HEADS)]
    yield from emit(jobs, seg96, 1.0 / MLA_QK, mla_t, MLA_ROPE // 2)


def _prologue(layer, x, g1, w1, gqn, wuq, gkvn, wuk, wuv, lane_gain, rope, seg):
    B, S, _ = x.shape
    nrb = S // ROW_BLOCK
    params = (g1, w1, gqn, wuq, gkvn, wuk, wuv, lane_gain)
    return pl.pallas_call(
        _prologue_kernel,
        grid=(nrb, B),
        in_specs=[pl.BlockSpec((1, ROW_BLOCK, D_MODEL), lambda i, b: (b, i, 0))]
        + [_layer_spec(p, layer) for p in params]
        + [pl.BlockSpec((9, ROW_BLOCK, LANES), lambda i, b: (0, i, 0)),
           _const_spec(seg.shape)],
        out_specs=pl.BlockSpec((1, ROW_BLOCK, O_END), lambda i, b: (b, i, 0)),
        out_shape=jax.ShapeDtypeStruct((B, S, O_END), BF16),
        compiler_params=pltpu.CompilerParams(
            dimension_semantics=("arbitrary", "arbitrary"),
            vmem_limit_bytes=VMEM_LIMIT),
        name="prologue",
    )(x, g1, w1, gqn, wuq, gkvn, wuk, wuv, lane_gain, rope, seg)


def _lane_tiles(a):
    return [a[:, j * LANES:(j + 1) * LANES] for j in range(a.shape[1] // LANES)]


def _attention_pipeline(q_ref, groups, o_ref, scratch, combine):
    kt_scr, s_scr, m_scr, p_scr, v_scr = scratch
    seq = q_ref.shape[1]
    nblk = seq // Q_BLOCK
    nchunk = seq // KEY_CHUNK
    units = [(g, i * Q_BLOCK, h) for g in range(len(groups)) for i in range(nblk)
             for h in range(2)]

    def score_pass(g, r0, h):
        q_col, kt_slot = groups[g][h], groups[g][2][h]
        q = q_ref[0, r0:r0 + Q_BLOCK, q_col:q_col + LANES]
        mx = None
        for c in range(nchunk):
            ks = slice(c * KEY_CHUNK, (c + 1) * KEY_CHUNK)
            s = jnp.dot(q, kt_scr[kt_slot, :, ks], preferred_element_type=F32)
            s_scr[h, :, ks] = s
            part = functools.reduce(jnp.maximum, _lane_tiles(s))
            mx = part if mx is None else jnp.maximum(mx, part)
            yield
        m_scr[h] = jnp.broadcast_to(jnp.max(mx, axis=-1, keepdims=True), (Q_BLOCK, LANES))
        yield

    def value_pass(g, h, outs):
        mb = m_scr[h]
        for c in range(nchunk):
            ks = slice(c * KEY_CHUNK, (c + 1) * KEY_CHUNK)
            p = jnp.concatenate([jnp.exp2(t - mb) for t in _lane_tiles(s_scr[h, :, ks])],
                                axis=1)
            p_scr[h, :, ks] = p.astype(BF16)
            yield
        acc = jnp.dot(p_scr[h], v_scr[groups[g][3]], preferred_element_type=F32)
        outs.append(acc[:, 0:LANES] / acc[:, LANES:2 * LANES])
        yield

    _interleave(score_pass(*units[0]))
    outs = []
    for n, (g, r0, h) in enumerate(units):
        nxt = [score_pass(*units[n + 1])] if n + 1 < len(units) else []
        _interleave(value_pass(g, h, outs), *nxt)
        if h == 1:
            oc = groups[g][4]
            o_ref[0, r0:r0 + Q_BLOCK, oc:oc + LANES] = combine(*outs).astype(BF16)
            outs = []


def _pick_halves(o0, o1):
    low = lax.broadcasted_iota(jnp.int32, o0.shape, 1) < HALF
    return jnp.where(low, o0, o1)


def _stage_values(v_ref, v_scr):
    seq = v_ref.shape[1]
    for j in range(v_scr.shape[0]):
        v_scr[j, :, 0:LANES] = v_ref[0, :, j * LANES:(j + 1) * LANES]
        v_scr[j, :, LANES:2 * LANES] = jnp.ones((seq, LANES), BF16)


def _stage_half_keys(k_ref, kt_scr):
    for j in range(kt_scr.shape[0] // 2):
        kt = k_ref[0, :, j * LANES:(j + 1) * LANES].T
        low = lax.broadcasted_iota(jnp.int32, kt.shape, 0) < HALF
        zero = jnp.zeros_like(kt)
        kt_scr[2 * j] = jnp.where(low, kt, zero)
        kt_scr[2 * j + 1] = jnp.where(low, zero, kt)


def _mla_attn_kernel(q_ref, k_ref, v_ref, o_ref, *scratch):
    kt_scr, v_scr = scratch[0], scratch[4]
    for j in range(kt_scr.shape[0]):
        kt_scr[j] = k_ref[0, :, j * LANES:(j + 1) * LANES].T
    _stage_values(v_ref, v_scr)
    groups = [(2 * p * LANES, (2 * p + 1) * LANES, (2 * p, 2 * p + 1), p, p * LANES)
              for p in range(v_scr.shape[0])]
    _attention_pipeline(q_ref, groups, o_ref, scratch, _pick_halves)


def _gqa_attn_kernel(q_ref, k_ref, v_ref, o_ref, *scratch):
    _stage_half_keys(k_ref, scratch[0])
    _stage_values(v_ref, scratch[4])
    groups = [(g * LANES, g * LANES, (0, 1), 0, g * LANES)
              for g in range(q_ref.shape[2] // LANES)]
    _attention_pipeline(q_ref, groups, o_ref, scratch, _pick_halves)


def _diff_attn_kernel(lv_ref, og_ref, q_ref, k_ref, v_ref, o_ref, *scratch, lambda_init):
    _stage_half_keys(k_ref, scratch[0])
    _stage_values(v_ref, scratch[4])
    groups = [(j * LANES, j * LANES, (2 * j, 2 * j + 1), j, j * LANES)
              for j in range(scratch[4].shape[0])]
    lv = lv_ref[...]
    lam = (jnp.exp(jnp.sum(lv[0:1] * lv[1:2], axis=-1, keepdims=True))
           - jnp.exp(jnp.sum(lv[2:3] * lv[3:4], axis=-1, keepdims=True))
           + lambda_init)
    out_gain = og_ref[...] * (1.0 - lambda_init)

    def combine(o0, o1):
        return _row_rms(o0 - lam * o1, out_gain)

    _attention_pipeline(q_ref, groups, o_ref, scratch, combine)


def _attn_scratch(seq, key_slots, value_slots):
    return [pltpu.VMEM((key_slots, LANES, seq), BF16),
            pltpu.VMEM((2, Q_BLOCK, seq), F32),
            pltpu.VMEM((2, Q_BLOCK, LANES), F32),
            pltpu.VMEM((2, Q_BLOCK, seq), BF16),
            pltpu.VMEM((value_slots, seq, 2 * LANES), BF16)]


def _attn_params(grid_rank):
    return pltpu.CompilerParams(dimension_semantics=("arbitrary",) * grid_rank,
                                vmem_limit_bytes=VMEM_LIMIT)


def _mla_attention(qkv):
    B, S, _ = qkv.shape
    qk_w, v_w = 2 * GROUPS_PER_STEP * LANES, GROUPS_PER_STEP * LANES
    return pl.pallas_call(
        _mla_attn_kernel,
        grid=(B, MLA_HEADS // 2 // GROUPS_PER_STEP),
        in_specs=[
            pl.BlockSpec((1, S, qk_w), lambda b, p: (b, 0, O_QA // qk_w + p)),
            pl.BlockSpec((1, S, qk_w), lambda b, p: (b, 0, O_KA // qk_w + p)),
            pl.BlockSpec((1, S, v_w), lambda b, p: (b, 0, O_VA // v_w + p)),
        ],
        out_specs=pl.BlockSpec((1, S, v_w), lambda b, p: (b, 0, p)),
        out_shape=jax.ShapeDtypeStruct((B, S, MLA_HEADS * MLA_V), BF16),
        scratch_shapes=_attn_scratch(S, 2 * GROUPS_PER_STEP, GROUPS_PER_STEP),
        compiler_params=_attn_params(2),
        name="mla_attention",
    )(qkv, qkv, qkv)


def _diff_attention(layer, qkv, lvec, out_gain):
    B, S, _ = qkv.shape
    lambda_init = 0.8 - 0.6 * math.exp(-0.3 * layer)
    w = GROUPS_PER_STEP * LANES
    return pl.pallas_call(
        functools.partial(_diff_attn_kernel, lambda_init=lambda_init),
        grid=(B, DIFF_HEADS // GROUPS_PER_STEP),
        in_specs=[
            _layer_spec(lvec, layer), _layer_spec(out_gain, layer),
            pl.BlockSpec((1, S, w), lambda b, h: (b, 0, O_QD // w + h)),
            pl.BlockSpec((1, S, w), lambda b, h: (b, 0, O_KD // w + h)),
            pl.BlockSpec((1, S, w), lambda b, h: (b, 0, O_VD // w + h)),
        ],
        out_specs=pl.BlockSpec((1, S, w), lambda b, h: (b, 0, h)),
        out_shape=jax.ShapeDtypeStruct((B, S, DIFF_HEADS * DIFF_V), BF16),
        scratch_shapes=_attn_scratch(S, 2 * GROUPS_PER_STEP, GROUPS_PER_STEP),
        compiler_params=_attn_params(2),
        name="diff_attention",
    )(lvec, out_gain, qkv, qkv, qkv)


def _gqa_attention(qkv):
    B, S, _ = qkv.shape
    q_w = GROUPS_PER_STEP * LANES
    return pl.pallas_call(
        _gqa_attn_kernel,
        grid=(B, GQA_GROUP // GROUPS_PER_STEP),
        in_specs=[
            pl.BlockSpec((1, S, q_w), lambda b, g: (b, 0, O_QG // q_w + g)),
            pl.BlockSpec((1, S, LANES), lambda b, g: (b, 0, O_KG // LANES)),
            pl.BlockSpec((1, S, LANES), lambda b, g: (b, 0, O_VG // LANES)),
        ],
        out_specs=pl.BlockSpec((1, S, q_w), lambda b, g: (b, 0, g)),
        out_shape=jax.ShapeDtypeStruct((B, S, GQA_HEADS * GQA_DIM), BF16),
        scratch_shapes=_attn_scratch(S, 2, 1),
        compiler_params=_attn_params(2),
        name="gqa_attention",
    )(qkv, qkv, qkv)


def _epilogue_kernel(x_ref, oa_ref, ob_ref, oc_ref, wo_ref, g2_ref, wgu_ref, wd_ref, y_ref):
    for r in range(ROW_BLOCK // SUB_ROWS):
        rs = slice(r * SUB_ROWS, (r + 1) * SUB_ROWS)
        mix = jnp.concatenate([oa_ref[0, rs], ob_ref[0, rs], oc_ref[0, rs]], axis=1)
        x1 = x_ref[0, rs] + jnp.dot(mix, wo_ref[...], preferred_element_type=F32)
        xn = _row_rms(x1, g2_ref[...]).astype(BF16)
        gate = jnp.dot(xn, wgu_ref[:, 0:FFN_HIDDEN], preferred_element_type=F32)
        up = jnp.dot(xn, wgu_ref[:, FFN_HIDDEN:2 * FFN_HIDDEN], preferred_element_type=F32)
        act = (gate * (1.0 / (1.0 + jnp.exp(-gate))) * up).astype(BF16)
        y_ref[0, rs] = x1 + jnp.dot(act, wd_ref[...], preferred_element_type=F32)


def _epilogue(layer, x, oa, ob, oc, wo, g2, wgu, wd):
    B, S, _ = x.shape
    nrb = S // ROW_BLOCK

    def rows(width):
        return pl.BlockSpec((1, ROW_BLOCK, width), lambda b, i: (b, i, 0))

    return pl.pallas_call(
        _epilogue_kernel,
        grid=(B, nrb),
        in_specs=[rows(D_MODEL), rows(oa.shape[-1]), rows(ob.shape[-1]), rows(oc.shape[-1])]
        + [_layer_spec(p, layer) for p in (wo, g2, wgu, wd)],
        out_specs=rows(D_MODEL),
        out_shape=jax.ShapeDtypeStruct(x.shape, F32),
        compiler_params=pltpu.CompilerParams(
            dimension_semantics=("arbitrary", "arbitrary"),
            vmem_limit_bytes=VMEM_LIMIT),
        name="epilogue",
    )(x, oa, ob, oc, wo, g2, wgu, wd)


def _rope_angles(pos, dim):
    inv = (1.0 / (ROPE_THETA ** (np.arange(0, dim, 2, dtype=np.float32) / dim))
           ).astype(np.float32)
    ang = pos.astype(np.float32)[:, None] * inv[None, :]
    return np.cos(ang), np.sin(ang)


def _rope_tables(seq):
    pos = np.arange(seq, dtype=np.int32)
    cm, sm = _rope_angles(pos, MLA_ROPE)
    cd, sd = _rope_angles(pos, DIFF_QK)
    cr, sr = _rope_angles(pos // GRID_W, GQA_DIM // 2)
    cc, sc = _rope_angles(pos % GRID_W, GQA_DIM // 2)
    one = lambda w: np.ones((seq, w), np.float32)
    zero = lambda w: np.zeros((seq, w), np.float32)
    cat = lambda parts: np.concatenate(parts, axis=-1)
    mla = [cat([one(64), cm, cm, one(32)]),
           cat([zero(64), zero(16), sm, zero(32)]),
           cat([zero(64), -sm, zero(16), zero(32)])]
    dif = [cat([cd, cd] * 2),
           cat([zero(32), sd] * 2),
           cat([-sd, zero(32)] * 2)]
    gqa = [cat([cr, cr, cc, cc] * 2),
           cat([zero(16), sr, zero(16), sc] * 2),
           cat([-sr, zero(16), -sc, zero(16)] * 2)]
    return jnp.asarray(np.stack(mla + dif + gqa).astype(np.float32))


def _segment_ones():
    lane = np.arange(2 * LANES)
    same_tile = (lane[:, None] // LANES) == (lane[None, :] // LANES)
    seg96 = same_tile & ((lane[:, None] % LANES) < MLA_QK)
    seg64 = (lane[:, None] // HALF) == (lane[None, :] // HALF)
    return jnp.asarray(np.stack([seg96, seg64]).astype(np.float32)).astype(BF16)


_GQA_HEAD_ORDER = (0, 3, 1, 4, 2, 5)


def kernel(x, attn_norm, w_in, mla_q_norm, mla_w_uq, mla_kv_norm, mla_w_ukv,
           mla_q_gain, mla_k_gain, diff_q_gain, diff_k_gain, diff_lq1, diff_lk1,
           diff_lq2, diff_lk2, diff_out_gain, gqa_q_gain, gqa_k_gain, w_o,
           ffn_norm, w_gate_up, w_down):
    B, S, _ = x.shape
    L = DEPTH
    rope = _rope_tables(S)
    seg = _segment_ones()
    order = jnp.array(_GQA_HEAD_ORDER)

    zc = lambda w: jnp.zeros((L, D_MODEL, w), F32)
    gq = w_in[:, :, 2208:2592].reshape(L, D_MODEL, GQA_HEADS, GQA_DIM)[:, :, order]
    w1 = jnp.concatenate(
        [w_in[:, :, 0:640], zc(64), w_in[:, :, 640:672], zc(32), w_in[:, :, 672:2208],
         gq.reshape(L, D_MODEL, GQA_HEADS * GQA_DIM), w_in[:, :, 2592:2848]],
        axis=-1).astype(BF16)
    wuq = jnp.pad(mla_w_uq.reshape(L, MLA_Q_RANK, MLA_HEADS, MLA_QK),
                  ((0, 0), (0, 0), (0, 0), (0, LANES - MLA_QK))
                  ).reshape(L, MLA_Q_RANK, MLA_HEADS * LANES).astype(BF16)
    ukv = mla_w_ukv.reshape(L, MLA_KV_RANK, MLA_HEADS, MLA_NOPE + MLA_V)
    wuk = jnp.pad(ukv[..., :MLA_NOPE], ((0, 0), (0, 0), (0, 0), (0, LANES - MLA_NOPE))
                  ).reshape(L, MLA_KV_RANK, MLA_HEADS * LANES).astype(BF16)
    wuv = ukv[..., MLA_NOPE:].reshape(L, MLA_KV_RANK, MLA_HEADS * MLA_V).astype(BF16)
    n_ab = MLA_HEADS * MLA_V + DIFF_HEADS * DIFF_V
    woc = w_o[:, n_ab:].reshape(L, GQA_HEADS, GQA_DIM, D_MODEL)[:, order]
    wo = jnp.concatenate([w_o[:, 0:n_ab], woc.reshape(L, GQA_HEADS * GQA_DIM, D_MODEL)],
                         axis=1).astype(BF16)
    wgu = w_gate_up.astype(BF16)
    wd = w_down.astype(BF16)

    pad96 = lambda g: jnp.pad(g, ((0, 0), (0, LANES - MLA_QK)))
    twice = lambda g: jnp.concatenate([g, g], axis=-1)
    lane_gain = jnp.stack(
        [pad96(mla_q_gain) * (MLA_QK ** -0.5 * LOG2E), pad96(mla_k_gain),
         twice(diff_q_gain) * (DIFF_QK ** -0.5 * LOG2E), twice(diff_k_gain),
         twice(gqa_q_gain) * (GQA_DIM ** -0.5 * LOG2E), twice(gqa_k_gain),
         jnp.zeros((L, LANES), F32), jnp.zeros((L, LANES), F32)], axis=1)
    pad64 = lambda g: jnp.pad(g, ((0, 0), (0, LANES - DIFF_QK)))
    lvec = jnp.stack([pad64(diff_lq1), pad64(diff_lk1), pad64(diff_lq2), pad64(diff_lk2)]
                     + [jnp.zeros((L, LANES), F32)] * 4, axis=1)

    row = lambda g: g[:, None, :]
    for l in range(L):
        qkv = _prologue(l, x, row(attn_norm), w1, row(mla_q_norm), wuq, row(mla_kv_norm),
                        wuk, wuv, lane_gain, rope, seg)
        oa = _mla_attention(qkv)
        ob = _diff_attention(l, qkv, lvec, row(diff_out_gain))
        oc = _gqa_attention(qkv)
        x = _epilogue(l, x, oa, ob, oc, wo, row(ffn_norm), wgu, wd)
    return x
```

```python
import functools
import math

import numpy as np
import jax
import jax.numpy as jnp
from jax import lax
from jax.experimental import pallas as pl
from jax.experimental.pallas import tpu as pltpu

D_MODEL = 1024
DEPTH = 2
MLA_HEADS = 6
MLA_Q_RANK = 384
MLA_KV_RANK = 256
MLA_NOPE = 64
MLA_ROPE = 32
MLA_QK = MLA_NOPE + MLA_ROPE
MLA_V = 64
DIFF_HEADS = 4
DIFF_QK = 64
DIFF_V = 128
GQA_HEADS = 6
GQA_KV_HEADS = 2
GQA_GROUP = GQA_HEADS // GQA_KV_HEADS
GQA_DIM = 64
GRID_W = 64
ROPE_THETA = 10000.0
EPS = 1e-6
FFN_HIDDEN = 2816

LANES = 128
HALF = 64
HEAD_BLOCK = 32
LOG2E = math.log2(math.e)

H_CQ, H_CKV, H_KR, H_DQ, H_DK, H_DV, H_GQ, H_GK, H_GV, H_END = (
    0, 384, 640, 768, 1280, 1792, 2304, 2688, 2816, 2944)
O_QA, O_KA, O_QD, O_KD, O_VD, O_VA, O_QG, O_KG, O_VG, O_END = (
    0, 768, 1536, 2048, 2560, 3072, 3456, 3840, 3968, 4096)

PROLOGUE_ROWS = 1024
ROW_BLOCK = 512
SUB_ROWS = 256
Q_BLOCK = 256
GROUPS_PER_STEP = 1
KEY_CHUNK = 256
VMEM_LIMIT = 56 * 1024 * 1024

F32 = jnp.float32
BF16 = jnp.bfloat16


def _const_spec(shape):
    nd = len(shape)
    return pl.BlockSpec(shape, lambda *_: (0,) * nd, pipeline_mode=pl.Buffered(1))


def _layer_spec(arr, layer):
    nd = arr.ndim
    return pl.BlockSpec((None,) + arr.shape[1:], lambda *_: (layer,) + (0,) * (nd - 1),
                        pipeline_mode=pl.Buffered(1))


def _interleave(*gens):
    live = list(gens)
    while live:
        for g in list(live):
            if next(g, _DONE) is _DONE:
                live.remove(g)


_DONE = object()


def _norm_rope_pair(ta, tb, gain_a, gain_b, seg2, dim, cos, sin):
    sq = jnp.concatenate([ta * ta, tb * tb], axis=1).astype(BF16)
    ss = jnp.dot(sq, seg2, preferred_element_type=F32)
    outs = []
    for t, gain, s in ((ta, gain_a, ss[:, 0:LANES]), (tb, gain_b, ss[:, LANES:2 * LANES])):
        rinv = lax.rsqrt(s + EPS * dim)
        u = t * gain
        r = u * cos + pltpu.roll(u, HALF, 1) * sin
        outs.append((r * rinv).astype(BF16))
    return outs


def _row_rms(v, gain):
    ms = jnp.mean(v * v, axis=-1, keepdims=True)
    return v * lax.rsqrt(ms + EPS) * gain


def _prologue_kernel(x_ref, g1_ref, w1_ref, gqn_ref, wuq_ref, gkvn_ref, wuk_ref,
                     wuv_ref, lg_ref, rope_ref, seg_ref, out_ref):
    nsub = PROLOGUE_ROWS // SUB_ROWS
    rows = [slice(r * SUB_ROWS, (r + 1) * SUB_ROWS) for r in range(nsub)]
    hs = {}

    def project(r):
        xn = _row_rms(x_ref[0, rows[r]], g1_ref[...]).astype(BF16)
        hs[r] = jnp.dot(xn, w1_ref[...], preferred_element_type=F32)
        yield

    def finish(r):
        return _prologue_finish(rows[r], hs.pop(r), gqn_ref, wuq_ref, gkvn_ref, wuk_ref,
                                wuv_ref, lg_ref, rope_ref, seg_ref, out_ref)

    _interleave(project(0))
    for r in range(nsub):
        _interleave(finish(r), *([project(r + 1)] if r + 1 < nsub else []))


def _prologue_finish(rs, h, gqn_ref, wuq_ref, gkvn_ref, wuk_ref, wuv_ref, lg_ref,
                     rope_ref, seg_ref, out_ref):
    seg_tile = seg_ref[0]
    seg_half = seg_ref[1]
    mla_t = (rope_ref[0, rs], rope_ref[1, rs])
    dif_t = (rope_ref[2, rs], rope_ref[3, rs])
    gqa_t = (rope_ref[4, rs], rope_ref[5, rs])
    lg = lg_ref[...]

    def tile(a, off, j):
        return a[:, off + j * LANES:off + (j + 1) * LANES]

    def emit(jobs, seg2, dim, tables):
        for (ta, ga, ca), (tb, gb, cb) in zip(jobs[0::2], jobs[1::2]):
            ra, rb = _norm_rope_pair(ta, tb, ga, gb, seg2, dim, *tables)
            out_ref[0, rs, ca:ca + LANES] = ra
            out_ref[0, rs, cb:cb + LANES] = rb
            yield

    cq = _row_rms(h[:, H_CQ:H_CKV], gqn_ref[...]).astype(BF16)
    qa = jnp.dot(cq, wuq_ref[...], preferred_element_type=F32)
    ckv = _row_rms(h[:, H_CKV:H_KR], gkvn_ref[...]).astype(BF16)
    kn = jnp.dot(ckv, wuk_ref[...], preferred_element_type=F32)
    va = jnp.dot(ckv, wuv_ref[...], preferred_element_type=F32)
    kr = h[:, H_KR:H_DQ]
    out_ref[0, rs, O_VA:O_VA + MLA_HEADS * MLA_V] = va.astype(BF16)
    yield

    jobs = [(tile(h, H_DQ, j), lg[2:3], O_QD + j * LANES) for j in range(DIFF_HEADS)]
    jobs += [(tile(h, H_DK, j), lg[3:4], O_KD + j * LANES) for j in range(DIFF_HEADS)]
    yield from emit(jobs, seg_half, DIFF_QK, dif_t)
    out_ref[0, rs, O_VD:O_VD + DIFF_HEADS * DIFF_V] = h[:, H_DV:H_GQ].astype(BF16)

    jobs = [(tile(h, H_GQ, j), lg[4:5], O_QG + j * LANES) for j in range(GQA_GROUP)]
    jobs += [(tile(h, H_GK, 0), lg[5:6], O_KG)]
    yield from emit(jobs, seg_half, GQA_DIM, gqa_t)
    out_ref[0, rs, O_VG:O_END] = h[:, H_GV:H_END].astype(BF16)

    jobs = [(tile(qa, 0, j), lg[0:1], O_QA + j * LANES) for j in range(MLA_HEADS)]
    jobs += [(tile(kn, 0, j) + kr, lg[1:2], O_KA + j * LANES) for j in range(MLA_HEADS)]
    yield from emit(jobs, seg_tile, MLA_QK, mla_t)


def _prologue(layer, x, g1, w1, gqn, wuq, gkvn, wuk, wuv, lane_gain, rope, seg):
    B, S, _ = x.shape
    nrb = S // PROLOGUE_ROWS
    params = (g1, w1, gqn, wuq, gkvn, wuk, wuv, lane_gain)
    return pl.pallas_call(
        _prologue_kernel,
        grid=(nrb, B),
        in_specs=[pl.BlockSpec((1, PROLOGUE_ROWS, D_MODEL), lambda i, b: (b, i, 0))]
        + [_layer_spec(p, layer) for p in params]
        + [pl.BlockSpec((rope.shape[0], PROLOGUE_ROWS, LANES), lambda i, b: (0, i, 0)),
           _const_spec(seg.shape)],
        out_specs=pl.BlockSpec((1, PROLOGUE_ROWS, O_END), lambda i, b: (b, i, 0)),
        out_shape=jax.ShapeDtypeStruct((B, S, O_END), BF16),
        compiler_params=pltpu.CompilerParams(
            dimension_semantics=("arbitrary", "arbitrary"),
            vmem_limit_bytes=VMEM_LIMIT),
        name="prologue",
    )(x, g1, w1, gqn, wuq, gkvn, wuk, wuv, lane_gain, rope, seg)


def _lane_tiles(a):
    return [a[:, j * LANES:(j + 1) * LANES] for j in range(a.shape[1] // LANES)]


def _attention_pipeline(q_ref, groups, o_ref, scratch, combine):
    kt_scr, s_scr, m_scr, p_scr, v_scr = scratch
    seq = q_ref.shape[1]
    nblk = seq // Q_BLOCK
    nchunk = seq // KEY_CHUNK
    units = [(g, i * Q_BLOCK, h) for g in range(len(groups)) for i in range(nblk)
             for h in range(2)]

    def score_pass(g, r0, h):
        q_col, kt_slot = groups[g][h], groups[g][2][h]
        q = q_ref[0, r0:r0 + Q_BLOCK, q_col:q_col + LANES]
        mx = None
        for c in range(nchunk):
            ks = slice(c * KEY_CHUNK, (c + 1) * KEY_CHUNK)
            s = jnp.dot(q, kt_scr[kt_slot, :, ks], preferred_element_type=F32)
            s_scr[h, :, ks] = s
            part = functools.reduce(jnp.maximum, _lane_tiles(s))
            mx = part if mx is None else jnp.maximum(mx, part)
            yield
        m_scr[h] = jnp.broadcast_to(jnp.max(mx, axis=-1, keepdims=True), (Q_BLOCK, LANES))
        yield

    def value_pass(g, h, outs):
        mb = m_scr[h]
        for c in range(nchunk):
            ks = slice(c * KEY_CHUNK, (c + 1) * KEY_CHUNK)
            p = jnp.concatenate([jnp.exp2(t - mb) for t in _lane_tiles(s_scr[h, :, ks])],
                                axis=1)
            p_scr[h, :, ks] = p.astype(BF16)
            yield
        acc = jnp.dot(p_scr[h], v_scr[groups[g][3]], preferred_element_type=F32)
        outs.append(acc[:, 0:LANES] / acc[:, LANES:2 * LANES])
        yield

    _interleave(score_pass(*units[0]))
    outs = []
    for n, (g, r0, h) in enumerate(units):
        nxt = [score_pass(*units[n + 1])] if n + 1 < len(units) else []
        _interleave(value_pass(g, h, outs), *nxt)
        if h == 1:
            oc = groups[g][4]
            o_ref[0, r0:r0 + Q_BLOCK, oc:oc + LANES] = combine(*outs).astype(BF16)
            outs = []


def _pick_halves(o0, o1):
    low = lax.broadcasted_iota(jnp.int32, o0.shape, 1) < HALF
    return jnp.where(low, o0, o1)


def _stage_values(v_ref, v_scr):
    seq = v_ref.shape[1]
    for j in range(v_scr.shape[0]):
        v_scr[j, :, 0:LANES] = v_ref[0, :, j * LANES:(j + 1) * LANES]
        v_scr[j, :, LANES:2 * LANES] = jnp.ones((seq, LANES), BF16)


def _stage_half_keys(k_ref, kt_scr):
    for j in range(kt_scr.shape[0] // 2):
        kt = k_ref[0, :, j * LANES:(j + 1) * LANES].T
        low = (lax.broadcasted_iota(jnp.int32, kt.shape, 0) & HEAD_BLOCK) == 0
        zero = jnp.zeros_like(kt)
        kt_scr[2 * j] = jnp.where(low, kt, zero)
        kt_scr[2 * j + 1] = jnp.where(low, zero, kt)


def _mla_attn_kernel(q_ref, k_ref, v_ref, o_ref, *scratch):
    kt_scr, v_scr = scratch[0], scratch[4]
    for j in range(kt_scr.shape[0]):
        kt_scr[j] = k_ref[0, :, j * LANES:(j + 1) * LANES].T
    _stage_values(v_ref, v_scr)
    groups = [(2 * p * LANES, (2 * p + 1) * LANES, (2 * p, 2 * p + 1), p, p * LANES)
              for p in range(v_scr.shape[0])]
    _attention_pipeline(q_ref, groups, o_ref, scratch, _pick_halves)


def _gqa_attn_kernel(q_ref, k_ref, v_ref, o_ref, *scratch):
    _stage_half_keys(k_ref, scratch[0])
    _stage_values(v_ref, scratch[4])
    groups = [(g * LANES, g * LANES, (0, 1), 0, g * LANES)
              for g in range(q_ref.shape[2] // LANES)]
    _attention_pipeline(q_ref, groups, o_ref, scratch, _pick_halves)


def _diff_attn_kernel(lv_ref, og_ref, q_ref, k_ref, v_ref, o_ref, *scratch, lambda_init):
    _stage_half_keys(k_ref, scratch[0])
    _stage_values(v_ref, scratch[4])
    groups = [(j * LANES, j * LANES, (2 * j, 2 * j + 1), j, j * LANES)
              for j in range(scratch[4].shape[0])]
    lv = lv_ref[...]
    lam = (jnp.exp(jnp.sum(lv[0:1] * lv[1:2], axis=-1, keepdims=True))
           - jnp.exp(jnp.sum(lv[2:3] * lv[3:4], axis=-1, keepdims=True))
           + lambda_init)
    out_gain = og_ref[...] * (1.0 - lambda_init)

    def combine(o0, o1):
        return _row_rms(o0 - lam * o1, out_gain)

    _attention_pipeline(q_ref, groups, o_ref, scratch, combine)


def _attn_scratch(seq, key_slots, value_slots):
    return [pltpu.VMEM((key_slots, LANES, seq), BF16),
            pltpu.VMEM((2, Q_BLOCK, seq), F32),
            pltpu.VMEM((2, Q_BLOCK, LANES), F32),
            pltpu.VMEM((2, Q_BLOCK, seq), BF16),
            pltpu.VMEM((value_slots, seq, 2 * LANES), BF16)]


def _attn_params(grid_rank):
    return pltpu.CompilerParams(dimension_semantics=("arbitrary",) * grid_rank,
                                vmem_limit_bytes=VMEM_LIMIT)


def _mla_attention(qkv):
    B, S, _ = qkv.shape
    qk_w, v_w = 2 * GROUPS_PER_STEP * LANES, GROUPS_PER_STEP * LANES
    return pl.pallas_call(
        _mla_attn_kernel,
        grid=(B, MLA_HEADS // 2 // GROUPS_PER_STEP),
        in_specs=[
            pl.BlockSpec((1, S, qk_w), lambda b, p: (b, 0, O_QA // qk_w + p)),
            pl.BlockSpec((1, S, qk_w), lambda b, p: (b, 0, O_KA // qk_w + p)),
            pl.BlockSpec((1, S, v_w), lambda b, p: (b, 0, O_VA // v_w + p)),
        ],
        out_specs=pl.BlockSpec((1, S, v_w), lambda b, p: (b, 0, p)),
        out_shape=jax.ShapeDtypeStruct((B, S, MLA_HEADS * MLA_V), BF16),
        scratch_shapes=_attn_scratch(S, 2 * GROUPS_PER_STEP, GROUPS_PER_STEP),
        compiler_params=_attn_params(2),
        name="mla_attention",
    )(qkv, qkv, qkv)


def _diff_attention(layer, qkv, lvec, out_gain):
    B, S, _ = qkv.shape
    lambda_init = 0.8 - 0.6 * math.exp(-0.3 * layer)
    w = GROUPS_PER_STEP * LANES
    return pl.pallas_call(
        functools.partial(_diff_attn_kernel, lambda_init=lambda_init),
        grid=(B, DIFF_HEADS // GROUPS_PER_STEP),
        in_specs=[
            _layer_spec(lvec, layer), _layer_spec(out_gain, layer),
            pl.BlockSpec((1, S, w), lambda b, h: (b, 0, O_QD // w + h)),
            pl.BlockSpec((1, S, w), lambda b, h: (b, 0, O_KD // w + h)),
            pl.BlockSpec((1, S, w), lambda b, h: (b, 0, O_VD // w + h)),
        ],
        out_specs=pl.BlockSpec((1, S, w), lambda b, h: (b, 0, h)),
        out_shape=jax.ShapeDtypeStruct((B, S, DIFF_HEADS * DIFF_V), BF16),
        scratch_shapes=_attn_scratch(S, 2 * GROUPS_PER_STEP, GROUPS_PER_STEP),
        compiler_params=_attn_params(2),
        name="diff_attention",
    )(lvec, out_gain, qkv, qkv, qkv)


def _gqa_attention(qkv):
    B, S, _ = qkv.shape
    q_w = GROUPS_PER_STEP * LANES
    return pl.pallas_call(
        _gqa_attn_kernel,
        grid=(B, GQA_GROUP // GROUPS_PER_STEP),
        in_specs=[
            pl.BlockSpec((1, S, q_w), lambda b, g: (b, 0, O_QG // q_w + g)),
            pl.BlockSpec((1, S, LANES), lambda b, g: (b, 0, O_KG // LANES)),
            pl.BlockSpec((1, S, LANES), lambda b, g: (b, 0, O_VG // LANES)),
        ],
        out_specs=pl.BlockSpec((1, S, q_w), lambda b, g: (b, 0, g)),
        out_shape=jax.ShapeDtypeStruct((B, S, GQA_HEADS * GQA_DIM), BF16),
        scratch_shapes=_attn_scratch(S, 2, 1),
        compiler_params=_attn_params(2),
        name="gqa_attention",
    )(qkv, qkv, qkv)


def _epilogue_kernel(x_ref, oa_ref, ob_ref, oc_ref, wo_ref, g2_ref, wgu_ref, wd_ref, y_ref):
    for r in range(ROW_BLOCK // SUB_ROWS):
        rs = slice(r * SUB_ROWS, (r + 1) * SUB_ROWS)
        mix = jnp.concatenate([oa_ref[0, rs], ob_ref[0, rs], oc_ref[0, rs]], axis=1)
        x1 = x_ref[0, rs] + jnp.dot(mix, wo_ref[...], preferred_element_type=F32)
        xn = _row_rms(x1, g2_ref[...]).astype(BF16)
        gate = jnp.dot(xn, wgu_ref[:, 0:FFN_HIDDEN], preferred_element_type=F32)
        up = jnp.dot(xn, wgu_ref[:, FFN_HIDDEN:2 * FFN_HIDDEN], preferred_element_type=F32)
        act = (gate * (1.0 / (1.0 + jnp.exp(-gate))) * up).astype(BF16)
        y_ref[0, rs] = x1 + jnp.dot(act, wd_ref[...], preferred_element_type=F32)


def _epilogue(layer, x, oa, ob, oc, wo, g2, wgu, wd):
    B, S, _ = x.shape
    nrb = S // ROW_BLOCK

    def rows(width):
        return pl.BlockSpec((1, ROW_BLOCK, width), lambda b, i: (b, i, 0))

    return pl.pallas_call(
        _epilogue_kernel,
        grid=(B, nrb),
        in_specs=[rows(D_MODEL), rows(oa.shape[-1]), rows(ob.shape[-1]), rows(oc.shape[-1])]
        + [_layer_spec(p, layer) for p in (wo, g2, wgu, wd)],
        out_specs=rows(D_MODEL),
        out_shape=jax.ShapeDtypeStruct(x.shape, F32),
        compiler_params=pltpu.CompilerParams(
            dimension_semantics=("arbitrary", "arbitrary"),
            vmem_limit_bytes=VMEM_LIMIT),
        name="epilogue",
    )(x, oa, ob, oc, wo, g2, wgu, wd)


def _rope_angles(pos, dim):
    inv = (1.0 / (ROPE_THETA ** (np.arange(0, dim, 2, dtype=np.float32) / dim))
           ).astype(np.float32)
    ang = pos.astype(np.float32)[:, None] * inv[None, :]
    return np.cos(ang), np.sin(ang)


def _rope_tables(seq):
    pos = np.arange(seq, dtype=np.int32)
    cm, sm = _rope_angles(pos, MLA_ROPE)
    cd, sd = _rope_angles(pos, DIFF_QK)
    cr, sr = _rope_angles(pos // GRID_W, GQA_DIM // 2)
    cc, sc = _rope_angles(pos % GRID_W, GQA_DIM // 2)
    one = lambda w: np.ones((seq, w), np.float32)
    zero = lambda w: np.zeros((seq, w), np.float32)
    cat = lambda parts: np.concatenate(parts, axis=-1)
    mla = [cat([one(48), cm, one(48), cm]), cat([zero(48), -sm, zero(48), sm])]
    dif = [cat([cd] * 4), cat([-sd, -sd, sd, sd])]
    gqa = [cat([cr, cc] * 4), cat([-sr, -sc, -sr, -sc, sr, sc, sr, sc])]
    return jnp.asarray(np.stack(mla + dif + gqa).astype(np.float32))


def _segment_ones():
    lane = np.arange(2 * LANES)
    same_tile = (lane[:, None] // LANES) == (lane[None, :] // LANES)
    same_head = (lane[:, None] // HEAD_BLOCK) % 2 == (lane[None, :] // HEAD_BLOCK) % 2
    seg = np.stack([same_tile, same_tile & same_head])
    return jnp.asarray(seg.astype(np.float32)).astype(BF16)


def _mla_tile(d, zeros):
    return jnp.concatenate([d[..., 0:48], d[..., 64:80], d[..., 48:64], zeros(32),
                            d[..., 80:96]], axis=-1)


def _diff_tile(d):
    s = d.shape[:-1]
    return jnp.swapaxes(d.reshape(s + (2, 2, 32)), -3, -2).reshape(s + (LANES,))


def _gqa_tile(d):
    s = d.shape[:-1]
    return jnp.moveaxis(d.reshape(s + (2, 2, 2, 16)), -2, -4).reshape(s + (LANES,))


_GQA_HEAD_ORDER = (0, 3, 1, 4, 2, 5)


def kernel(x, attn_norm, w_in, mla_q_norm, mla_w_uq, mla_kv_norm, mla_w_ukv,
           mla_q_gain, mla_k_gain, diff_q_gain, diff_k_gain, diff_lq1, diff_lk1,
           diff_lq2, diff_lk2, diff_out_gain, gqa_q_gain, gqa_k_gain, w_o,
           ffn_norm, w_gate_up, w_down):
    B, S, _ = x.shape
    L = DEPTH
    rope = _rope_tables(S)
    seg = _segment_ones()
    order = jnp.array(_GQA_HEAD_ORDER)

    zc = lambda w: jnp.zeros((L, D_MODEL, w), F32)
    tiles = lambda a, n: a.reshape(L, D_MODEL, n, LANES)
    flat = lambda a: a.reshape(L, D_MODEL, -1)
    gq = w_in[:, :, 2208:2592].reshape(L, D_MODEL, GQA_HEADS, GQA_DIM)[:, :, order]
    kr = w_in[:, :, 640:672]
    w1 = jnp.concatenate(
        [w_in[:, :, 0:640],
         zc(48), kr[..., 0:16], zc(48), kr[..., 16:32],
         flat(_diff_tile(tiles(w_in[:, :, 672:1184], DIFF_HEADS))),
         flat(_diff_tile(tiles(w_in[:, :, 1184:1696], DIFF_HEADS))),
         w_in[:, :, 1696:2208],
         flat(_gqa_tile(tiles(flat(gq), GQA_GROUP))),
         _gqa_tile(w_in[:, :, 2592:2720]), w_in[:, :, 2720:2848]],
        axis=-1).astype(BF16)
    uq = mla_w_uq.reshape(L, MLA_Q_RANK, MLA_HEADS, MLA_QK)
    wuq = _mla_tile(uq, lambda w: jnp.zeros(uq.shape[:-1] + (w,), F32)
                    ).reshape(L, MLA_Q_RANK, MLA_HEADS * LANES).astype(BF16)
    ukv = mla_w_ukv.reshape(L, MLA_KV_RANK, MLA_HEADS, MLA_NOPE + MLA_V)
    zk = lambda w: jnp.zeros(ukv.shape[:-1] + (w,), F32)
    wuk = jnp.concatenate([ukv[..., 0:48], zk(16), ukv[..., 48:64], zk(48)], axis=-1
                          ).reshape(L, MLA_KV_RANK, MLA_HEADS * LANES).astype(BF16)
    wuv = ukv[..., MLA_NOPE:].reshape(L, MLA_KV_RANK, MLA_HEADS * MLA_V).astype(BF16)
    n_ab = MLA_HEADS * MLA_V + DIFF_HEADS * DIFF_V
    woc = w_o[:, n_ab:].reshape(L, GQA_HEADS, GQA_DIM, D_MODEL)[:, order]
    wo = jnp.concatenate([w_o[:, 0:n_ab], woc.reshape(L, GQA_HEADS * GQA_DIM, D_MODEL)],
                         axis=1).astype(BF16)
    wgu = w_gate_up.astype(BF16)
    wd = w_down.astype(BF16)

    mla_g = lambda g: _mla_tile(g, lambda w: jnp.zeros((L, w), F32))
    twice = lambda g: jnp.concatenate([g, g], axis=-1)
    lane_gain = jnp.stack(
        [mla_g(mla_q_gain) * LOG2E, mla_g(mla_k_gain) * MLA_QK ** 0.5,
         _diff_tile(twice(diff_q_gain)) * LOG2E,
         _diff_tile(twice(diff_k_gain)) * DIFF_QK ** 0.5,
         _gqa_tile(twice(gqa_q_gain)) * LOG2E,
         _gqa_tile(twice(gqa_k_gain)) * GQA_DIM ** 0.5,
         jnp.zeros((L, LANES), F32), jnp.zeros((L, LANES), F32)], axis=1)
    pad64 = lambda g: jnp.pad(g, ((0, 0), (0, LANES - DIFF_QK)))
    lvec = jnp.stack([pad64(diff_lq1), pad64(diff_lk1), pad64(diff_lq2), pad64(diff_lk2)]
                     + [jnp.zeros((L, LANES), F32)] * 4, axis=1)

    row = lambda g: g[:, None, :]
    for l in range(L):
        qkv = _prologue(l, x, row(attn_norm), w1, row(mla_q_norm), wuq, row(mla_kv_norm),
                        wuk, wuv, lane_gain, rope, seg)
        oa = _mla_attention(qkv)
        ob = _diff_attention(l, qkv, lvec, row(diff_out_gain))
        oc = _gqa_attention(qkv)
        x = _epilogue(l, x, oa, ob, oc, wo, row(ffn_norm), wgu, wd)
    return x
```

```python
import functools
import math

import numpy as np
import jax
import jax.numpy as jnp
from jax import lax
from jax.experimental import pallas as pl
from jax.experimental.pallas import tpu as pltpu

D_MODEL = 1024
DEPTH = 2
MLA_HEADS = 6
MLA_Q_RANK = 384
MLA_KV_RANK = 256
MLA_NOPE = 64
MLA_ROPE = 32
MLA_QK = MLA_NOPE + MLA_ROPE
MLA_V = 64
DIFF_HEADS = 4
DIFF_QK = 64
DIFF_V = 128
GQA_HEADS = 6
GQA_KV_HEADS = 2
GQA_GROUP = GQA_HEADS // GQA_KV_HEADS
GQA_DIM = 64
GRID_W = 64
ROPE_THETA = 10000.0
EPS = 1e-6
FFN_HIDDEN = 2816

LANES = 128
HALF = 64
HEAD_BLOCK = 32
LOG2E = math.log2(math.e)

H_CQ, H_CKV, H_KR, H_DQ, H_DK, H_DV, H_GQ, H_GK, H_GV, H_END = (
    0, 384, 640, 768, 1280, 1792, 2304, 2688, 2816, 2944)
O_QA, O_KA, O_QD, O_KD, O_VD, O_VA, O_QG, O_KG, O_VG, O_END = (
    0, 768, 1536, 2048, 2560, 3072, 3456, 3840, 3968, 4096)

PROLOGUE_ROWS = 1024
ROW_BLOCK = 512
SUB_ROWS = 256
Q_BLOCK = 256
GROUPS_PER_STEP = 1
DIFF_HEADS_PER_STEP = 2
KEY_CHUNK = 256
VMEM_LIMIT = 56 * 1024 * 1024

F32 = jnp.float32
BF16 = jnp.bfloat16


def _const_spec(shape):
    nd = len(shape)
    return pl.BlockSpec(shape, lambda *_: (0,) * nd, pipeline_mode=pl.Buffered(1))


def _layer_spec(arr, layer):
    nd = arr.ndim
    return pl.BlockSpec((None,) + arr.shape[1:], lambda *_: (layer,) + (0,) * (nd - 1),
                        pipeline_mode=pl.Buffered(1))


def _interleave(*gens):
    live = list(gens)
    while live:
        for g in list(live):
            if next(g, _DONE) is _DONE:
                live.remove(g)


_DONE = object()


def _norm_rope_pair(ta, tb, gain_a, gain_b, seg2, dim, cos, sin):
    sq = jnp.concatenate([ta * ta, tb * tb], axis=1).astype(BF16)
    ss = jnp.dot(sq, seg2, preferred_element_type=F32)
    outs = []
    for t, gain, s in ((ta, gain_a, ss[:, 0:LANES]), (tb, gain_b, ss[:, LANES:2 * LANES])):
        rinv = lax.rsqrt(s + EPS * dim)
        u = t * gain
        r = u * cos + pltpu.roll(u, HALF, 1) * sin
        outs.append((r * rinv).astype(BF16))
    return outs


def _row_rms(v, gain):
    ms = jnp.mean(v * v, axis=-1, keepdims=True)
    return v * lax.rsqrt(ms + EPS) * gain


def _prologue_kernel(x_ref, g1_ref, w1_ref, gqn_ref, wuq_ref, gkvn_ref, wuk_ref,
                     wuv_ref, lg_ref, rope_ref, seg_ref, out_ref):
    nsub = PROLOGUE_ROWS // SUB_ROWS
    rows = [slice(r * SUB_ROWS, (r + 1) * SUB_ROWS) for r in range(nsub)]
    hs = {}

    def project(r):
        xn = _row_rms(x_ref[0, rows[r]], g1_ref[...]).astype(BF16)
        hs[r] = jnp.dot(xn, w1_ref[...], preferred_element_type=F32)
        yield

    def finish(r):
        return _prologue_finish(rows[r], hs.pop(r), gqn_ref, wuq_ref, gkvn_ref, wuk_ref,
                                wuv_ref, lg_ref, rope_ref, seg_ref, out_ref)

    _interleave(project(0))
    for r in range(nsub):
        _interleave(finish(r), *([project(r + 1)] if r + 1 < nsub else []))


def _prologue_finish(rs, h, gqn_ref, wuq_ref, gkvn_ref, wuk_ref, wuv_ref, lg_ref,
                     rope_ref, seg_ref, out_ref):
    seg_tile = seg_ref[0]
    seg_half = seg_ref[1]
    mla_t = (rope_ref[0, rs], rope_ref[1, rs])
    dif_t = (rope_ref[2, rs], rope_ref[3, rs])
    gqa_t = (rope_ref[4, rs], rope_ref[5, rs])
    lg = lg_ref[...]

    def tile(a, off, j):
        return a[:, off + j * LANES:off + (j + 1) * LANES]

    def emit(jobs, seg2, dim, tables):
        for (ta, ga, ca), (tb, gb, cb) in zip(jobs[0::2], jobs[1::2]):
            ra, rb = _norm_rope_pair(ta, tb, ga, gb, seg2, dim, *tables)
            out_ref[0, rs, ca:ca + LANES] = ra
            out_ref[0, rs, cb:cb + LANES] = rb
            yield

    cq = _row_rms(h[:, H_CQ:H_CKV], gqn_ref[...]).astype(BF16)
    qa = jnp.dot(cq, wuq_ref[...], preferred_element_type=F32)
    ckv = _row_rms(h[:, H_CKV:H_KR], gkvn_ref[...]).astype(BF16)
    kn = jnp.dot(ckv, wuk_ref[...], preferred_element_type=F32)
    va = jnp.dot(ckv, wuv_ref[...], preferred_element_type=F32)
    kr = h[:, H_KR:H_DQ]
    out_ref[0, rs, O_VA:O_VA + MLA_HEADS * MLA_V] = va.astype(BF16)
    yield

    jobs = [(tile(h, H_DQ, j), lg[2:3], O_QD + j * LANES) for j in range(DIFF_HEADS)]
    jobs += [(tile(h, H_DK, j), lg[3:4], O_KD + j * LANES) for j in range(DIFF_HEADS)]
    yield from emit(jobs, seg_half, DIFF_QK, dif_t)
    out_ref[0, rs, O_VD:O_VD + DIFF_HEADS * DIFF_V] = h[:, H_DV:H_GQ].astype(BF16)

    jobs = [(tile(h, H_GQ, j), lg[4:5], O_QG + j * LANES) for j in range(GQA_GROUP)]
    jobs += [(tile(h, H_GK, 0), lg[5:6], O_KG)]
    yield from emit(jobs, seg_half, GQA_DIM, gqa_t)
    out_ref[0, rs, O_VG:O_END] = h[:, H_GV:H_END].astype(BF16)

    jobs = [(tile(qa, 0, j), lg[0:1], O_QA + j * LANES) for j in range(MLA_HEADS)]
    jobs += [(tile(kn, 0, j) + kr, lg[1:2], O_KA + j * LANES) for j in range(MLA_HEADS)]
    yield from emit(jobs, seg_tile, MLA_QK, mla_t)


def _prologue(layer, x, g1, w1, gqn, wuq, gkvn, wuk, wuv, lane_gain, rope, seg):
    B, S, _ = x.shape
    nrb = S // PROLOGUE_ROWS
    params = (g1, w1, gqn, wuq, gkvn, wuk, wuv, lane_gain)
    return pl.pallas_call(
        _prologue_kernel,
        grid=(nrb, B),
        in_specs=[pl.BlockSpec((1, PROLOGUE_ROWS, D_MODEL), lambda i, b: (b, i, 0))]
        + [_layer_spec(p, layer) for p in params]
        + [pl.BlockSpec((rope.shape[0], PROLOGUE_ROWS, LANES), lambda i, b: (0, i, 0)),
           _const_spec(seg.shape)],
        out_specs=pl.BlockSpec((1, PROLOGUE_ROWS, O_END), lambda i, b: (b, i, 0)),
        out_shape=jax.ShapeDtypeStruct((B, S, O_END), BF16),
        compiler_params=pltpu.CompilerParams(
            dimension_semantics=("arbitrary", "arbitrary"),
            vmem_limit_bytes=VMEM_LIMIT),
        name="prologue",
    )(x, g1, w1, gqn, wuq, gkvn, wuk, wuv, lane_gain, rope, seg)


def _lane_tiles(a):
    return [a[:, j * LANES:(j + 1) * LANES] for j in range(a.shape[1] // LANES)]


def _attention_pipeline(q_ref, groups, o_ref, scratch, combine):
    kt_scr, s_scr, m_scr, p_scr, v_scr = scratch
    seq = q_ref.shape[1]
    nblk = seq // Q_BLOCK
    nchunk = seq // KEY_CHUNK
    units = [(g, i * Q_BLOCK, h) for g in range(len(groups)) for i in range(nblk)
             for h in range(2)]

    def score_pass(g, r0, h):
        q_col, kt_slot = groups[g][h], groups[g][2][h]
        q = q_ref[0, r0:r0 + Q_BLOCK, q_col:q_col + LANES]
        mx = None
        for c in range(nchunk):
            ks = slice(c * KEY_CHUNK, (c + 1) * KEY_CHUNK)
            s = jnp.dot(q, kt_scr[kt_slot, :, ks], preferred_element_type=F32)
            s_scr[h, :, ks] = s
            part = functools.reduce(jnp.maximum, _lane_tiles(s))
            mx = part if mx is None else jnp.maximum(mx, part)
            yield
        m_scr[h] = jnp.broadcast_to(jnp.max(mx, axis=-1, keepdims=True), (Q_BLOCK, LANES))
        yield

    def value_pass(g, h, outs):
        mb = m_scr[h]
        for c in range(nchunk):
            ks = slice(c * KEY_CHUNK, (c + 1) * KEY_CHUNK)
            p = jnp.concatenate([jnp.exp2(t - mb) for t in _lane_tiles(s_scr[h, :, ks])],
                                axis=1)
            p_scr[h, :, ks] = p.astype(BF16)
            yield
        acc = jnp.dot(p_scr[h], v_scr[groups[g][3]], preferred_element_type=F32)
        outs.append(acc[:, 0:LANES] / acc[:, LANES:2 * LANES])
        yield

    _interleave(score_pass(*units[0]))
    outs = []
    for n, (g, r0, h) in enumerate(units):
        nxt = [score_pass(*units[n + 1])] if n + 1 < len(units) else []
        _interleave(value_pass(g, h, outs), *nxt)
        if h == 1:
            oc = groups[g][4]
            o_ref[0, r0:r0 + Q_BLOCK, oc:oc + LANES] = combine(*outs).astype(BF16)
            outs = []


def _pick_halves(o0, o1):
    low = lax.broadcasted_iota(jnp.int32, o0.shape, 1) < HALF
    return jnp.where(low, o0, o1)


def _stage_values(v_ref, v_scr):
    seq = v_ref.shape[1]
    for j in range(v_scr.shape[0]):
        v_scr[j, :, 0:LANES] = v_ref[0, :, j * LANES:(j + 1) * LANES]
        v_scr[j, :, LANES:2 * LANES] = jnp.ones((seq, LANES), BF16)


def _stage_half_keys(k_ref, kt_scr):
    for j in range(kt_scr.shape[0] // 2):
        kt = k_ref[0, :, j * LANES:(j + 1) * LANES].T
        low = (lax.broadcasted_iota(jnp.int32, kt.shape, 0) & HEAD_BLOCK) == 0
        zero = jnp.zeros_like(kt)
        kt_scr[2 * j] = jnp.where(low, kt, zero)
        kt_scr[2 * j + 1] = jnp.where(low, zero, kt)


def _mla_attn_kernel(q_ref, k_ref, v_ref, o_ref, *scratch):
    kt_scr, v_scr = scratch[0], scratch[4]
    for j in range(kt_scr.shape[0]):
        kt_scr[j] = k_ref[0, :, j * LANES:(j + 1) * LANES].T
    _stage_values(v_ref, v_scr)
    groups = [(2 * p * LANES, (2 * p + 1) * LANES, (2 * p, 2 * p + 1), p, p * LANES)
              for p in range(v_scr.shape[0])]
    _attention_pipeline(q_ref, groups, o_ref, scratch, _pick_halves)


def _gqa_attn_kernel(q_ref, k_ref, v_ref, o_ref, *scratch):
    _stage_half_keys(k_ref, scratch[0])
    _stage_values(v_ref, scratch[4])
    groups = [(g * LANES, g * LANES, (0, 1), 0, g * LANES)
              for g in range(q_ref.shape[2] // LANES)]
    _attention_pipeline(q_ref, groups, o_ref, scratch, _pick_halves)


def _diff_attn_kernel(lv_ref, og_ref, q_ref, k_ref, v_ref, o_ref, *scratch, lambda_init):
    _stage_half_keys(k_ref, scratch[0])
    _stage_values(v_ref, scratch[4])
    groups = [(j * LANES, j * LANES, (2 * j, 2 * j + 1), j, j * LANES)
              for j in range(scratch[4].shape[0])]
    lv = lv_ref[...]
    lam = (jnp.exp(jnp.sum(lv[0:1] * lv[1:2], axis=-1, keepdims=True))
           - jnp.exp(jnp.sum(lv[2:3] * lv[3:4], axis=-1, keepdims=True))
           + lambda_init)
    out_gain = og_ref[...] * (1.0 - lambda_init)

    def combine(o0, o1):
        return _row_rms(o0 - lam * o1, out_gain)

    _attention_pipeline(q_ref, groups, o_ref, scratch, combine)


def _attn_scratch(seq, key_slots, value_slots):
    return [pltpu.VMEM((key_slots, LANES, seq), BF16),
            pltpu.VMEM((2, Q_BLOCK, seq), F32),
            pltpu.VMEM((2, Q_BLOCK, LANES), F32),
            pltpu.VMEM((2, Q_BLOCK, seq), BF16),
            pltpu.VMEM((value_slots, seq, 2 * LANES), BF16)]


def _attn_params(grid_rank):
    return pltpu.CompilerParams(dimension_semantics=("arbitrary",) * grid_rank,
                                vmem_limit_bytes=VMEM_LIMIT)


def _mla_attention(qkv):
    B, S, _ = qkv.shape
    qk_w, v_w = 2 * GROUPS_PER_STEP * LANES, GROUPS_PER_STEP * LANES
    return pl.pallas_call(
        _mla_attn_kernel,
        grid=(B, MLA_HEADS // 2 // GROUPS_PER_STEP),
        in_specs=[
            pl.BlockSpec((1, S, qk_w), lambda b, p: (b, 0, O_QA // qk_w + p)),
            pl.BlockSpec((1, S, qk_w), lambda b, p: (b, 0, O_KA // qk_w + p)),
            pl.BlockSpec((1, S, v_w), lambda b, p: (b, 0, O_VA // v_w + p)),
        ],
        out_specs=pl.BlockSpec((1, S, v_w), lambda b, p: (b, 0, p)),
        out_shape=jax.ShapeDtypeStruct((B, S, MLA_HEADS * MLA_V), BF16),
        scratch_shapes=_attn_scratch(S, 2 * GROUPS_PER_STEP, GROUPS_PER_STEP),
        compiler_params=_attn_params(2),
        name="mla_attention",
    )(qkv, qkv, qkv)


def _diff_attention(layer, qkv, lvec, out_gain):
    B, S, _ = qkv.shape
    lambda_init = 0.8 - 0.6 * math.exp(-0.3 * layer)
    w = DIFF_HEADS_PER_STEP * LANES
    return pl.pallas_call(
        functools.partial(_diff_attn_kernel, lambda_init=lambda_init),
        grid=(B, DIFF_HEADS // DIFF_HEADS_PER_STEP),
        in_specs=[
            _layer_spec(lvec, layer), _layer_spec(out_gain, layer),
            pl.BlockSpec((1, S, w), lambda b, h: (b, 0, O_QD // w + h)),
            pl.BlockSpec((1, S, w), lambda b, h: (b, 0, O_KD // w + h)),
            pl.BlockSpec((1, S, w), lambda b, h: (b, 0, O_VD // w + h)),
        ],
        out_specs=pl.BlockSpec((1, S, w), lambda b, h: (b, 0, h)),
        out_shape=jax.ShapeDtypeStruct((B, S, DIFF_HEADS * DIFF_V), BF16),
        scratch_shapes=_attn_scratch(S, 2 * DIFF_HEADS_PER_STEP, DIFF_HEADS_PER_STEP),
        compiler_params=_attn_params(2),
        name="diff_attention",
    )(lvec, out_gain, qkv, qkv, qkv)


def _gqa_attention(qkv):
    B, S, _ = qkv.shape
    q_w = GROUPS_PER_STEP * LANES
    return pl.pallas_call(
        _gqa_attn_kernel,
        grid=(B, GQA_GROUP // GROUPS_PER_STEP),
        in_specs=[
            pl.BlockSpec((1, S, q_w), lambda b, g: (b, 0, O_QG // q_w + g)),
            pl.BlockSpec((1, S, LANES), lambda b, g: (b, 0, O_KG // LANES)),
            pl.BlockSpec((1, S, LANES), lambda b, g: (b, 0, O_VG // LANES)),
        ],
        out_specs=pl.BlockSpec((1, S, q_w), lambda b, g: (b, 0, g)),
        out_shape=jax.ShapeDtypeStruct((B, S, GQA_HEADS * GQA_DIM), BF16),
        scratch_shapes=_attn_scratch(S, 2, 1),
        compiler_params=_attn_params(2),
        name="gqa_attention",
    )(qkv, qkv, qkv)


def _epilogue_kernel(x_ref, oa_ref, ob_ref, oc_ref, wo_ref, g2_ref, wgu_ref, wd_ref, y_ref):
    for r in range(ROW_BLOCK // SUB_ROWS):
        rs = slice(r * SUB_ROWS, (r + 1) * SUB_ROWS)
        mix = jnp.concatenate([oa_ref[0, rs], ob_ref[0, rs], oc_ref[0, rs]], axis=1)
        x1 = x_ref[0, rs] + jnp.dot(mix, wo_ref[...], preferred_element_type=F32)
        xn = _row_rms(x1, g2_ref[...]).astype(BF16)
        gate = jnp.dot(xn, wgu_ref[:, 0:FFN_HIDDEN], preferred_element_type=F32)
        up = jnp.dot(xn, wgu_ref[:, FFN_HIDDEN:2 * FFN_HIDDEN], preferred_element_type=F32)
        act = (gate * (1.0 / (1.0 + jnp.exp(-gate))) * up).astype(BF16)
        y_ref[0, rs] = x1 + jnp.dot(act, wd_ref[...], preferred_element_type=F32)


def _epilogue(layer, x, oa, ob, oc, wo, g2, wgu, wd):
    B, S, _ = x.shape
    nrb = S // ROW_BLOCK

    def rows(width):
        return pl.BlockSpec((1, ROW_BLOCK, width), lambda b, i: (b, i, 0))

    return pl.pallas_call(
        _epilogue_kernel,
        grid=(B, nrb),
        in_specs=[rows(D_MODEL), rows(oa.shape[-1]), rows(ob.shape[-1]), rows(oc.shape[-1])]
        + [_layer_spec(p, layer) for p in (wo, g2, wgu, wd)],
        out_specs=rows(D_MODEL),
        out_shape=jax.ShapeDtypeStruct(x.shape, F32),
        compiler_params=pltpu.CompilerParams(
            dimension_semantics=("arbitrary", "arbitrary"),
            vmem_limit_bytes=VMEM_LIMIT),
        name="epilogue",
    )(x, oa, ob, oc, wo, g2, wgu, wd)


def _rope_angles(pos, dim):
    inv = (1.0 / (ROPE_THETA ** (np.arange(0, dim, 2, dtype=np.float32) / dim))
           ).astype(np.float32)
    ang = pos.astype(np.float32)[:, None] * inv[None, :]
    return np.cos(ang), np.sin(ang)


def _rope_tables(seq):
    pos = np.arange(seq, dtype=np.int32)
    cm, sm = _rope_angles(pos, MLA_ROPE)
    cd, sd = _rope_angles(pos, DIFF_QK)
    cr, sr = _rope_angles(pos // GRID_W, GQA_DIM // 2)
    cc, sc = _rope_angles(pos % GRID_W, GQA_DIM // 2)
    one = lambda w: np.ones((seq, w), np.float32)
    zero = lambda w: np.zeros((seq, w), np.float32)
    cat = lambda parts: np.concatenate(parts, axis=-1)
    mla = [cat([one(48), cm, one(48), cm]), cat([zero(48), -sm, zero(48), sm])]
    dif = [cat([cd] * 4), cat([-sd, -sd, sd, sd])]
    gqa = [cat([cr, cc] * 4), cat([-sr, -sc, -sr, -sc, sr, sc, sr, sc])]
    return jnp.asarray(np.stack(mla + dif + gqa).astype(np.float32))


def _segment_ones():
    lane = np.arange(2 * LANES)
    same_tile = (lane[:, None] // LANES) == (lane[None, :] // LANES)
    same_head = (lane[:, None] // HEAD_BLOCK) % 2 == (lane[None, :] // HEAD_BLOCK) % 2
    seg = np.stack([same_tile, same_tile & same_head])
    return jnp.asarray(seg.astype(np.float32)).astype(BF16)


def _mla_tile(d, zeros):
    return jnp.concatenate([d[..., 0:48], d[..., 64:80], d[..., 48:64], zeros(32),
                            d[..., 80:96]], axis=-1)


def _diff_tile(d):
    s = d.shape[:-1]
    return jnp.swapaxes(d.reshape(s + (2, 2, 32)), -3, -2).reshape(s + (LANES,))


def _gqa_tile(d):
    s = d.shape[:-1]
    return jnp.moveaxis(d.reshape(s + (2, 2, 2, 16)), -2, -4).reshape(s + (LANES,))


_GQA_HEAD_ORDER = (0, 3, 1, 4, 2, 5)


def kernel(x, attn_norm, w_in, mla_q_norm, mla_w_uq, mla_kv_norm, mla_w_ukv,
           mla_q_gain, mla_k_gain, diff_q_gain, diff_k_gain, diff_lq1, diff_lk1,
           diff_lq2, diff_lk2, diff_out_gain, gqa_q_gain, gqa_k_gain, w_o,
           ffn_norm, w_gate_up, w_down):
    B, S, _ = x.shape
    L = DEPTH
    rope = _rope_tables(S)
    seg = _segment_ones()
    order = jnp.array(_GQA_HEAD_ORDER)

    wb = w_in.astype(BF16)
    zc = lambda w: jnp.zeros((L, D_MODEL, w), BF16)
    tiles = lambda a, n: a.reshape(L, D_MODEL, n, LANES)
    flat = lambda a: a.reshape(L, D_MODEL, -1)
    gq = wb[:, :, 2208:2592].reshape(L, D_MODEL, GQA_HEADS, GQA_DIM)[:, :, order]
    kr = wb[:, :, 640:672]
    w1 = jnp.concatenate(
        [wb[:, :, 0:640],
         zc(48), kr[..., 0:16], zc(48), kr[..., 16:32],
         flat(_diff_tile(tiles(wb[:, :, 672:1184], DIFF_HEADS))),
         flat(_diff_tile(tiles(wb[:, :, 1184:1696], DIFF_HEADS))),
         wb[:, :, 1696:2208],
         flat(_gqa_tile(tiles(flat(gq), GQA_GROUP))),
         _gqa_tile(wb[:, :, 2592:2720]), wb[:, :, 2720:2848]],
        axis=-1)
    uq = mla_w_uq.reshape(L, MLA_Q_RANK, MLA_HEADS, MLA_QK)
    wuq = _mla_tile(uq, lambda w: jnp.zeros(uq.shape[:-1] + (w,), F32)
                    ).reshape(L, MLA_Q_RANK, MLA_HEADS * LANES).astype(BF16)
    ukv = mla_w_ukv.reshape(L, MLA_KV_RANK, MLA_HEADS, MLA_NOPE + MLA_V)
    zk = lambda w: jnp.zeros(ukv.shape[:-1] + (w,), F32)
    wuk = jnp.concatenate([ukv[..., 0:48], zk(16), ukv[..., 48:64], zk(48)], axis=-1
                          ).reshape(L, MLA_KV_RANK, MLA_HEADS * LANES).astype(BF16)
    wuv = ukv[..., MLA_NOPE:].reshape(L, MLA_KV_RANK, MLA_HEADS * MLA_V).astype(BF16)
    n_ab = MLA_HEADS * MLA_V + DIFF_HEADS * DIFF_V
    wob = w_o.astype(BF16)
    woc = wob[:, n_ab:].reshape(L, GQA_HEADS, GQA_DIM, D_MODEL)[:, order]
    wo = jnp.concatenate([wob[:, 0:n_ab], woc.reshape(L, GQA_HEADS * GQA_DIM, D_MODEL)],
                         axis=1)
    wgu = w_gate_up.astype(BF16)
    wd = w_down.astype(BF16)

    mla_g = lambda g: _mla_tile(g, lambda w: jnp.zeros((L, w), F32))
    twice = lambda g: jnp.concatenate([g, g], axis=-1)
    lane_gain = jnp.stack(
        [mla_g(mla_q_gain) * LOG2E, mla_g(mla_k_gain) * MLA_QK ** 0.5,
         _diff_tile(twice(diff_q_gain)) * LOG2E,
         _diff_tile(twice(diff_k_gain)) * DIFF_QK ** 0.5,
         _gqa_tile(twice(gqa_q_gain)) * LOG2E,
         _gqa_tile(twice(gqa_k_gain)) * GQA_DIM ** 0.5,
         jnp.zeros((L, LANES), F32), jnp.zeros((L, LANES), F32)], axis=1)
    pad64 = lambda g: jnp.pad(g, ((0, 0), (0, LANES - DIFF_QK)))
    lvec = jnp.stack([pad64(diff_lq1), pad64(diff_lk1), pad64(diff_lq2), pad64(diff_lk2)]
                     + [jnp.zeros((L, LANES), F32)] * 4, axis=1)

    row = lambda g: g[:, None, :]
    for l in range(L):
        qkv = _prologue(l, x, row(attn_norm), w1, row(mla_q_norm), wuq, row(mla_kv_norm),
                        wuk, wuv, lane_gain, rope, seg)
        oa = _mla_attention(qkv)
        ob = _diff_attention(l, qkv, lvec, row(diff_out_gain))
        oc = _gqa_attention(qkv)
        x = _epilogue(l, x, oa, ob, oc, wo, row(ffn_norm), wgu, wd)
    return x
```

```python
import functools
import math

import numpy as np
import jax
import jax.numpy as jnp
from jax import lax
from jax.experimental import pallas as pl
from jax.experimental.pallas import tpu as pltpu

D_MODEL = 1024
DEPTH = 2
MLA_HEADS = 6
MLA_Q_RANK = 384
MLA_KV_RANK = 256
MLA_NOPE = 64
MLA_ROPE = 32
MLA_QK = MLA_NOPE + MLA_ROPE
MLA_V = 64
DIFF_HEADS = 4
DIFF_QK = 64
DIFF_V = 128
GQA_HEADS = 6
GQA_KV_HEADS = 2
GQA_GROUP = GQA_HEADS // GQA_KV_HEADS
GQA_DIM = 64
GRID_W = 64
ROPE_THETA = 10000.0
EPS = 1e-6
FFN_HIDDEN = 2816

LANES = 128
HALF = 64
HEAD_BLOCK = 32
LOG2E = math.log2(math.e)

H_CQ, H_CKV, H_KR, H_DQ, H_DK, H_DV, H_GQ, H_GK, H_GV, H_END = (
    0, 384, 640, 768, 1280, 1792, 2304, 2688, 2816, 2944)
O_QA, O_KA, O_QD, O_KD, O_VD, O_VA, O_QG, O_KG, O_VG, O_END = (
    0, 768, 1536, 2048, 2560, 3072, 3456, 3840, 3968, 4096)

PROLOGUE_ROWS = 1024
ROW_BLOCK = 512
SUB_ROWS = 256
Q_BLOCK = 256
ROPE_HALF = MLA_ROPE // 2
MLA_X1 = HALF - ROPE_HALF
MLA_X2 = LANES - ROPE_HALF
_MLA_PAD = MLA_X2 - (HALF + MLA_NOPE - MLA_X1)
KEY_CHUNK = 256
VMEM_LIMIT = 56 * 1024 * 1024

F32 = jnp.float32
BF16 = jnp.bfloat16


def _const_spec(shape):
    nd = len(shape)
    return pl.BlockSpec(shape, lambda *_: (0,) * nd, pipeline_mode=pl.Buffered(1))


def _layer_spec(arr, layer):
    nd = arr.ndim
    return pl.BlockSpec((None,) + arr.shape[1:], lambda *_: (layer,) + (0,) * (nd - 1),
                        pipeline_mode=pl.Buffered(1))


def _interleave(*gens):
    live = list(gens)
    while live:
        for g in list(live):
            if next(g, _DONE) is _DONE:
                live.remove(g)


_DONE = object()


def _norm_rope_pair(ta, tb, gain_a, gain_b, seg2, dim, cos, sin):
    sq = jnp.concatenate([ta * ta, tb * tb], axis=1).astype(BF16)
    ss = jnp.dot(sq, seg2, preferred_element_type=F32)
    outs = []
    for t, gain, s in ((ta, gain_a, ss[:, 0:LANES]), (tb, gain_b, ss[:, LANES:2 * LANES])):
        rinv = lax.rsqrt(s + EPS * dim)
        u = t * gain
        r = u * cos + pltpu.roll(u, HALF, 1) * sin
        outs.append((r * rinv).astype(BF16))
    return outs


def _row_rms(v, gain):
    ms = jnp.mean(v * v, axis=-1, keepdims=True)
    return v * lax.rsqrt(ms + EPS) * gain


def _prologue_kernel(x_ref, g1_ref, w1_ref, gqn_ref, wuq_ref, gkvn_ref, wuk_ref,
                     wuv_ref, lg_ref, rope_ref, seg_ref, out_ref):
    nsub = PROLOGUE_ROWS // SUB_ROWS
    rows = [slice(r * SUB_ROWS, (r + 1) * SUB_ROWS) for r in range(nsub)]
    hs = {}

    def project(r):
        xn = _row_rms(x_ref[0, rows[r]], g1_ref[...]).astype(BF16)
        hs[r] = jnp.dot(xn, w1_ref[...], preferred_element_type=F32)
        yield

    def finish(r):
        return _prologue_finish(rows[r], hs.pop(r), gqn_ref, wuq_ref, gkvn_ref, wuk_ref,
                                wuv_ref, lg_ref, rope_ref, seg_ref, out_ref)

    _interleave(project(0))
    for r in range(nsub):
        _interleave(finish(r), *([project(r + 1)] if r + 1 < nsub else []))


def _prologue_finish(rs, h, gqn_ref, wuq_ref, gkvn_ref, wuk_ref, wuv_ref, lg_ref,
                     rope_ref, seg_ref, out_ref):
    seg_tile = seg_ref[0]
    seg_half = seg_ref[1]
    mla_t = (rope_ref[0, rs], rope_ref[1, rs])
    dif_t = (rope_ref[2, rs], rope_ref[3, rs])
    gqa_t = (rope_ref[4, rs], rope_ref[5, rs])
    lg = lg_ref[...]

    def tile(a, off, j):
        return a[:, off + j * LANES:off + (j + 1) * LANES]

    def emit(jobs, seg2, dim, tables):
        for (ta, ga, ca), (tb, gb, cb) in zip(jobs[0::2], jobs[1::2]):
            ra, rb = _norm_rope_pair(ta, tb, ga, gb, seg2, dim, *tables)
            out_ref[0, rs, ca:ca + LANES] = ra
            out_ref[0, rs, cb:cb + LANES] = rb
            yield

    cq = _row_rms(h[:, H_CQ:H_CKV], gqn_ref[...]).astype(BF16)
    qa = jnp.dot(cq, wuq_ref[...], preferred_element_type=F32)
    ckv = _row_rms(h[:, H_CKV:H_KR], gkvn_ref[...]).astype(BF16)
    kn = jnp.dot(ckv, wuk_ref[...], preferred_element_type=F32)
    va = jnp.dot(ckv, wuv_ref[...], preferred_element_type=F32)
    kr = h[:, H_KR:H_DQ]
    out_ref[0, rs, O_VA:O_VA + MLA_HEADS * MLA_V] = va.astype(BF16)
    yield

    jobs = [(tile(h, H_DQ, j), lg[2:3], O_QD + j * LANES) for j in range(DIFF_HEADS)]
    jobs += [(tile(h, H_DK, j), lg[3:4], O_KD + j * LANES) for j in range(DIFF_HEADS)]
    yield from emit(jobs, seg_half, DIFF_QK, dif_t)
    out_ref[0, rs, O_VD:O_VD + DIFF_HEADS * DIFF_V] = h[:, H_DV:H_GQ].astype(BF16)

    jobs = [(tile(h, H_GQ, j), lg[4:5], O_QG + j * LANES) for j in range(GQA_GROUP)]
    jobs += [(tile(h, H_GK, 0), lg[5:6], O_KG)]
    yield from emit(jobs, seg_half, GQA_DIM, gqa_t)
    out_ref[0, rs, O_VG:O_END] = h[:, H_GV:H_END].astype(BF16)

    jobs = [(tile(qa, 0, j), lg[0:1], O_QA + j * LANES) for j in range(MLA_HEADS)]
    jobs += [(tile(kn, 0, j) + kr, lg[1:2], O_KA + j * LANES) for j in range(MLA_HEADS)]
    yield from emit(jobs, seg_tile, MLA_QK, mla_t)


def _prologue(layer, x, g1, w1, gqn, wuq, gkvn, wuk, wuv, lane_gain, rope, seg):
    B, S, _ = x.shape
    nrb = S // PROLOGUE_ROWS
    params = (g1, w1, gqn, wuq, gkvn, wuk, wuv, lane_gain)
    return pl.pallas_call(
        _prologue_kernel,
        grid=(nrb, B),
        in_specs=[pl.BlockSpec((1, PROLOGUE_ROWS, D_MODEL), lambda i, b: (b, i, 0))]
        + [_layer_spec(p, layer) for p in params]
        + [pl.BlockSpec((rope.shape[0], PROLOGUE_ROWS, LANES), lambda i, b: (0, i, 0)),
           _const_spec(seg.shape)],
        out_specs=pl.BlockSpec((1, PROLOGUE_ROWS, O_END), lambda i, b: (b, i, 0)),
        out_shape=jax.ShapeDtypeStruct((B, S, O_END), BF16),
        compiler_params=pltpu.CompilerParams(
            dimension_semantics=("arbitrary", "arbitrary"),
            vmem_limit_bytes=VMEM_LIMIT),
        name="prologue",
    )(x, g1, w1, gqn, wuq, gkvn, wuk, wuv, lane_gain, rope, seg)


def _lane_tiles(a):
    return [a[:, j * LANES:(j + 1) * LANES] for j in range(a.shape[1] // LANES)]


def _attention_pipeline(q_ref, groups, o_ref, scratch, combine):
    kt_scr, s_scr, m_scr, p_scr, v_scr = scratch
    seq = q_ref.shape[1]
    nblk = seq // Q_BLOCK
    nchunk = seq // KEY_CHUNK
    units = [(g, i * Q_BLOCK, h) for g in range(len(groups)) for i in range(nblk)
             for h in range(2)]

    def score_pass(g, r0, h):
        q_col, kt_slot = groups[g][h], groups[g][2][h]
        q = q_ref[0, r0:r0 + Q_BLOCK, q_col:q_col + LANES]
        mx = None
        for c in range(nchunk):
            ks = slice(c * KEY_CHUNK, (c + 1) * KEY_CHUNK)
            s = jnp.dot(q, kt_scr[kt_slot, :, ks], preferred_element_type=F32)
            s_scr[h, :, ks] = s
            part = functools.reduce(jnp.maximum, _lane_tiles(s))
            mx = part if mx is None else jnp.maximum(mx, part)
            yield
        m_scr[h] = jnp.broadcast_to(jnp.max(mx, axis=-1, keepdims=True), (Q_BLOCK, LANES))
        yield

    def value_pass(g, h, outs):
        mb = m_scr[h]
        for c in range(nchunk):
            ks = slice(c * KEY_CHUNK, (c + 1) * KEY_CHUNK)
            p = jnp.concatenate([jnp.exp2(t - mb) for t in _lane_tiles(s_scr[h, :, ks])],
                                axis=1)
            p_scr[h, :, ks] = p.astype(BF16)
            yield
        acc = jnp.dot(p_scr[h], v_scr[groups[g][3]], preferred_element_type=F32)
        outs.append(acc[:, 0:LANES] / acc[:, LANES:2 * LANES])
        yield

    _interleave(score_pass(*units[0]))
    outs = []
    for n, (g, r0, h) in enumerate(units):
        nxt = [score_pass(*units[n + 1])] if n + 1 < len(units) else []
        _interleave(value_pass(g, h, outs), *nxt)
        if h == 1:
            oc = groups[g][4]
            o_ref[0, r0:r0 + Q_BLOCK, oc:oc + LANES] = combine(*outs).astype(BF16)
            outs = []


def _pick_halves(o0, o1):
    low = lax.broadcasted_iota(jnp.int32, o0.shape, 1) < HALF
    return jnp.where(low, o0, o1)


def _stage_values(v_ref, v_scr):
    seq = v_ref.shape[1]
    for j in range(v_scr.shape[0]):
        v_scr[j, :, 0:LANES] = v_ref[0, :, j * LANES:(j + 1) * LANES]
        v_scr[j, :, LANES:2 * LANES] = jnp.ones((seq, LANES), BF16)


def _stage_half_keys(k_ref, kt_scr):
    for j in range(kt_scr.shape[0] // 2):
        kt = k_ref[0, :, j * LANES:(j + 1) * LANES].T
        low = (lax.broadcasted_iota(jnp.int32, kt.shape, 0) & HEAD_BLOCK) == 0
        zero = jnp.zeros_like(kt)
        kt_scr[2 * j] = jnp.where(low, kt, zero)
        kt_scr[2 * j + 1] = jnp.where(low, zero, kt)


def _mla_attn_kernel(q_ref, k_ref, v_ref, o_ref, *scratch):
    kt_scr, v_scr = scratch[0], scratch[4]
    for j in range(kt_scr.shape[0]):
        kt_scr[j] = k_ref[0, :, j * LANES:(j + 1) * LANES].T
    _stage_values(v_ref, v_scr)
    groups = [(2 * p * LANES, (2 * p + 1) * LANES, (2 * p, 2 * p + 1), p, p * LANES)
              for p in range(v_scr.shape[0])]
    _attention_pipeline(q_ref, groups, o_ref, scratch, _pick_halves)


def _gqa_attn_kernel(q_ref, k_ref, v_ref, o_ref, *scratch):
    _stage_half_keys(k_ref, scratch[0])
    _stage_values(v_ref, scratch[4])
    groups = [(g * LANES, g * LANES, (0, 1), 0, g * LANES)
              for g in range(q_ref.shape[2] // LANES)]
    _attention_pipeline(q_ref, groups, o_ref, scratch, _pick_halves)


def _diff_attn_kernel(lv_ref, og_ref, q_ref, k_ref, v_ref, o_ref, *scratch, lambda_init):
    _stage_half_keys(k_ref, scratch[0])
    _stage_values(v_ref, scratch[4])
    groups = [(j * LANES, j * LANES, (2 * j, 2 * j + 1), j, j * LANES)
              for j in range(scratch[4].shape[0])]
    lv = lv_ref[...]
    lam = (jnp.exp(jnp.sum(lv[0:1] * lv[1:2], axis=-1, keepdims=True))
           - jnp.exp(jnp.sum(lv[2:3] * lv[3:4], axis=-1, keepdims=True))
           + lambda_init)
    out_gain = og_ref[...] * (1.0 - lambda_init)

    def combine(o0, o1):
        return _row_rms(o0 - lam * o1, out_gain)

    _attention_pipeline(q_ref, groups, o_ref, scratch, combine)


def _attn_scratch(seq, key_slots, value_slots):
    return [pltpu.VMEM((key_slots, LANES, seq), BF16),
            pltpu.VMEM((2, Q_BLOCK, seq), F32),
            pltpu.VMEM((2, Q_BLOCK, LANES), F32),
            pltpu.VMEM((2, Q_BLOCK, seq), BF16),
            pltpu.VMEM((value_slots, seq, 2 * LANES), BF16)]


def _attn_params(grid_rank):
    return pltpu.CompilerParams(dimension_semantics=("arbitrary",) * grid_rank,
                                vmem_limit_bytes=VMEM_LIMIT)


def _mla_attention(qkv):
    B, S, _ = qkv.shape
    qk_w, v_w = 2 * LANES, LANES
    return pl.pallas_call(
        _mla_attn_kernel,
        grid=(B, MLA_HEADS // 2),
        in_specs=[
            pl.BlockSpec((1, S, qk_w), lambda b, p: (b, 0, O_QA // qk_w + p)),
            pl.BlockSpec((1, S, qk_w), lambda b, p: (b, 0, O_KA // qk_w + p)),
            pl.BlockSpec((1, S, v_w), lambda b, p: (b, 0, O_VA // v_w + p)),
        ],
        out_specs=pl.BlockSpec((1, S, v_w), lambda b, p: (b, 0, p)),
        out_shape=jax.ShapeDtypeStruct((B, S, MLA_HEADS * MLA_V), BF16),
        scratch_shapes=_attn_scratch(S, 2, 1),
        compiler_params=_attn_params(2),
        name="mla_attention",
    )(qkv, qkv, qkv)


def _diff_attention(layer, qkv, lvec, out_gain):
    B, S, _ = qkv.shape
    lambda_init = 0.8 - 0.6 * math.exp(-0.3 * layer)
    w = LANES
    return pl.pallas_call(
        functools.partial(_diff_attn_kernel, lambda_init=lambda_init),
        grid=(B, DIFF_HEADS),
        in_specs=[
            _layer_spec(lvec, layer), _layer_spec(out_gain, layer),
            pl.BlockSpec((1, S, w), lambda b, h: (b, 0, O_QD // w + h)),
            pl.BlockSpec((1, S, w), lambda b, h: (b, 0, O_KD // w + h)),
            pl.BlockSpec((1, S, w), lambda b, h: (b, 0, O_VD // w + h)),
        ],
        out_specs=pl.BlockSpec((1, S, w), lambda b, h: (b, 0, h)),
        out_shape=jax.ShapeDtypeStruct((B, S, DIFF_HEADS * DIFF_V), BF16),
        scratch_shapes=_attn_scratch(S, 2, 1),
        compiler_params=_attn_params(2),
        name="diff_attention",
    )(lvec, out_gain, qkv, qkv, qkv)


def _gqa_attention(qkv):
    B, S, _ = qkv.shape
    q_w = LANES
    return pl.pallas_call(
        _gqa_attn_kernel,
        grid=(B, GQA_GROUP),
        in_specs=[
            pl.BlockSpec((1, S, q_w), lambda b, g: (b, 0, O_QG // q_w + g)),
            pl.BlockSpec((1, S, LANES), lambda b, g: (b, 0, O_KG // LANES)),
            pl.BlockSpec((1, S, LANES), lambda b, g: (b, 0, O_VG // LANES)),
        ],
        out_specs=pl.BlockSpec((1, S, q_w), lambda b, g: (b, 0, g)),
        out_shape=jax.ShapeDtypeStruct((B, S, GQA_HEADS * GQA_DIM), BF16),
        scratch_shapes=_attn_scratch(S, 2, 1),
        compiler_params=_attn_params(2),
        name="gqa_attention",
    )(qkv, qkv, qkv)


def _epilogue_kernel(x_ref, oa_ref, ob_ref, oc_ref, wo_ref, g2_ref, wgu_ref, wd_ref, y_ref):
    mix = jnp.concatenate([oa_ref[0], ob_ref[0], oc_ref[0]], axis=1)
    x1 = x_ref[0] + jnp.dot(mix, wo_ref[...], preferred_element_type=F32)
    xn = _row_rms(x1, g2_ref[...]).astype(BF16)
    gate = jnp.dot(xn, wgu_ref[:, 0:FFN_HIDDEN], preferred_element_type=F32)
    up = jnp.dot(xn, wgu_ref[:, FFN_HIDDEN:2 * FFN_HIDDEN], preferred_element_type=F32)
    act = (gate * (1.0 / (1.0 + jnp.exp(-gate))) * up).astype(BF16)
    y_ref[0] = x1 + jnp.dot(act, wd_ref[...], preferred_element_type=F32)


def _epilogue(layer, x, oa, ob, oc, wo, g2, wgu, wd):
    B, S, _ = x.shape
    nrb = S // ROW_BLOCK

    def rows(width):
        return pl.BlockSpec((1, ROW_BLOCK, width), lambda b, i: (b, i, 0))

    return pl.pallas_call(
        _epilogue_kernel,
        grid=(B, nrb),
        in_specs=[rows(D_MODEL), rows(oa.shape[-1]), rows(ob.shape[-1]), rows(oc.shape[-1])]
        + [_layer_spec(p, layer) for p in (wo, g2, wgu, wd)],
        out_specs=rows(D_MODEL),
        out_shape=jax.ShapeDtypeStruct(x.shape, F32),
        compiler_params=pltpu.CompilerParams(
            dimension_semantics=("arbitrary", "arbitrary"),
            vmem_limit_bytes=VMEM_LIMIT),
        name="epilogue",
    )(x, oa, ob, oc, wo, g2, wgu, wd)


def _rope_angles(pos, dim):
    inv = (1.0 / (ROPE_THETA ** (np.arange(0, dim, 2, dtype=np.float32) / dim))
           ).astype(np.float32)
    ang = pos.astype(np.float32)[:, None] * inv[None, :]
    return np.cos(ang), np.sin(ang)


def _rope_tables(seq):
    pos = np.arange(seq, dtype=np.int32)
    cm, sm = _rope_angles(pos, MLA_ROPE)
    cd, sd = _rope_angles(pos, DIFF_QK)
    cr, sr = _rope_angles(pos // GRID_W, GQA_DIM // 2)
    cc, sc = _rope_angles(pos % GRID_W, GQA_DIM // 2)
    one = lambda w: np.ones((seq, w), np.float32)
    zero = lambda w: np.zeros((seq, w), np.float32)
    cat = lambda parts: np.concatenate(parts, axis=-1)
    gap = MLA_X2 - MLA_X1 - ROPE_HALF
    mla = [cat([one(MLA_X1), cm, one(gap), cm]), cat([zero(MLA_X1), -sm, zero(gap), sm])]
    dif = [cat([cd] * 4), cat([-sd, -sd, sd, sd])]
    gqa = [cat([cr, cc] * 4), cat([-sr, -sc, -sr, -sc, sr, sc, sr, sc])]
    return jnp.asarray(np.stack(mla + dif + gqa).astype(np.float32))


def _segment_ones():
    lane = np.arange(2 * LANES)
    same_tile = (lane[:, None] // LANES) == (lane[None, :] // LANES)
    same_head = (lane[:, None] // HEAD_BLOCK) % 2 == (lane[None, :] // HEAD_BLOCK) % 2
    seg = np.stack([same_tile, same_tile & same_head])
    return jnp.asarray(seg.astype(np.float32)).astype(BF16)


def _mla_tile(d, zeros):
    x1, x2 = d[..., MLA_NOPE:MLA_NOPE + ROPE_HALF], d[..., MLA_NOPE + ROPE_HALF:MLA_QK]
    return jnp.concatenate([d[..., 0:MLA_X1], x1, d[..., MLA_X1:MLA_NOPE], zeros(_MLA_PAD), x2],
                           axis=-1)


def _diff_tile(d):
    s = d.shape[:-1]
    return jnp.swapaxes(d.reshape(s + (2, 2, 32)), -3, -2).reshape(s + (LANES,))


def _gqa_tile(d):
    s = d.shape[:-1]
    return jnp.moveaxis(d.reshape(s + (2, 2, 2, 16)), -2, -4).reshape(s + (LANES,))


_GQA_HEAD_ORDER = (0, 3, 1, 4, 2, 5)


def kernel(x, attn_norm, w_in, mla_q_norm, mla_w_uq, mla_kv_norm, mla_w_ukv,
           mla_q_gain, mla_k_gain, diff_q_gain, diff_k_gain, diff_lq1, diff_lk1,
           diff_lq2, diff_lk2, diff_out_gain, gqa_q_gain, gqa_k_gain, w_o,
           ffn_norm, w_gate_up, w_down):
    B, S, _ = x.shape
    L = DEPTH
    rope = _rope_tables(S)
    seg = _segment_ones()
    order = jnp.array(_GQA_HEAD_ORDER)

    wb = w_in.astype(BF16)
    tiles = lambda a, n: a.reshape(L, D_MODEL, n, LANES)
    flat = lambda a: a.reshape(L, D_MODEL, -1)
    place = lambda a, tile_fn: jnp.dot(a, tile_fn(jnp.eye(LANES, dtype=BF16)))
    gq = wb[:, :, 2208:2592].reshape(L, D_MODEL, GQA_HEADS, GQA_DIM)[:, :, order]
    rope_lanes = np.zeros((MLA_ROPE, LANES), np.float32)
    rope_lanes[np.arange(ROPE_HALF), MLA_X1 + np.arange(ROPE_HALF)] = 1.0
    rope_lanes[ROPE_HALF + np.arange(ROPE_HALF), MLA_X2 + np.arange(ROPE_HALF)] = 1.0
    w1 = jnp.concatenate(
        [wb[:, :, 0:640],
         jnp.dot(wb[:, :, 640:672], jnp.asarray(rope_lanes).astype(BF16)),
         flat(place(tiles(wb[:, :, 672:1696], 2 * DIFF_HEADS), _diff_tile)),
         wb[:, :, 1696:2208],
         flat(place(tiles(flat(gq), GQA_GROUP), _gqa_tile)),
         place(wb[:, :, 2592:2720], _gqa_tile), wb[:, :, 2720:2848]],
        axis=-1)
    uq = mla_w_uq.reshape(L, MLA_Q_RANK, MLA_HEADS, MLA_QK)
    wuq = _mla_tile(uq, lambda w: jnp.zeros(uq.shape[:-1] + (w,), F32)
                    ).reshape(L, MLA_Q_RANK, MLA_HEADS * LANES).astype(BF16)
    ukv = mla_w_ukv.reshape(L, MLA_KV_RANK, MLA_HEADS, MLA_NOPE + MLA_V)
    zk = lambda w: jnp.zeros(ukv.shape[:-1] + (w,), F32)
    wuk = jnp.concatenate([ukv[..., 0:MLA_X1], zk(ROPE_HALF), ukv[..., MLA_X1:MLA_NOPE],
                           zk(_MLA_PAD + ROPE_HALF)], axis=-1
                          ).reshape(L, MLA_KV_RANK, MLA_HEADS * LANES).astype(BF16)
    wuv = ukv[..., MLA_NOPE:].reshape(L, MLA_KV_RANK, MLA_HEADS * MLA_V).astype(BF16)
    n_ab = MLA_HEADS * MLA_V + DIFF_HEADS * DIFF_V
    wob = w_o.astype(BF16)
    woc = wob[:, n_ab:].reshape(L, GQA_HEADS, GQA_DIM, D_MODEL)[:, order]
    wo = jnp.concatenate([wob[:, 0:n_ab], woc.reshape(L, GQA_HEADS * GQA_DIM, D_MODEL)],
                         axis=1)
    wgu = w_gate_up.astype(BF16)
    wd = w_down.astype(BF16)

    mla_g = lambda g: _mla_tile(g, lambda w: jnp.zeros((L, w), F32))
    twice = lambda g: jnp.concatenate([g, g], axis=-1)
    lane_gain = jnp.stack(
        [mla_g(mla_q_gain) * LOG2E, mla_g(mla_k_gain) * MLA_QK ** 0.5,
         _diff_tile(twice(diff_q_gain)) * LOG2E,
         _diff_tile(twice(diff_k_gain)) * DIFF_QK ** 0.5,
         _gqa_tile(twice(gqa_q_gain)) * LOG2E,
         _gqa_tile(twice(gqa_k_gain)) * GQA_DIM ** 0.5,
         jnp.zeros((L, LANES), F32), jnp.zeros((L, LANES), F32)], axis=1)
    pad64 = lambda g: jnp.pad(g, ((0, 0), (0, LANES - DIFF_QK)))
    lvec = jnp.stack([pad64(diff_lq1), pad64(diff_lk1), pad64(diff_lq2), pad64(diff_lk2)]
                     + [jnp.zeros((L, LANES), F32)] * 4, axis=1)

    row = lambda g: g[:, None, :]
    for l in range(L):
        qkv = _prologue(l, x, row(attn_norm), w1, row(mla_q_norm), wuq, row(mla_kv_norm),
                        wuk, wuv, lane_gain, rope, seg)
        oa = _mla_attention(qkv)
        ob = _diff_attention(l, qkv, lvec, row(diff_out_gain))
        oc = _gqa_attention(qkv)
        x = _epilogue(l, x, oa, ob, oc, wo, row(ffn_norm), wgu, wd)
    return x
```

```python
import functools
import math

import numpy as np
import jax
import jax.numpy as jnp
from jax import lax
from jax.experimental import pallas as pl
from jax.experimental.pallas import tpu as pltpu

D_MODEL = 1024
DEPTH = 2
MLA_HEADS = 6
MLA_Q_RANK = 384
MLA_KV_RANK = 256
MLA_NOPE = 64
MLA_ROPE = 32
MLA_QK = MLA_NOPE + MLA_ROPE
MLA_V = 64
DIFF_HEADS = 4
DIFF_QK = 64
DIFF_V = 128
GQA_HEADS = 6
GQA_KV_HEADS = 2
GQA_GROUP = GQA_HEADS // GQA_KV_HEADS
GQA_DIM = 64
GRID_W = 64
ROPE_THETA = 10000.0
EPS = 1e-6
FFN_HIDDEN = 2816

LANES = 128
HALF = 64
HEAD_BLOCK = 32
LOG2E = math.log2(math.e)

H_CQ, H_CKV, H_KR, H_DQ, H_DK, H_DV, H_GQ, H_GK, H_GV, H_END = (
    0, 384, 640, 768, 1280, 1792, 2304, 2688, 2816, 2944)
O_QA, O_KA, O_QD, O_KD, O_VD, O_VA, O_QG, O_KG, O_VG, O_END = (
    0, 768, 1536, 2048, 2560, 3072, 3456, 3840, 3968, 4096)

PROLOGUE_ROWS = 1024
ROW_BLOCK = 1024
SUB_ROWS = 256
Q_BLOCK = 256
ROPE_HALF = MLA_ROPE // 2
MLA_X1 = HALF - ROPE_HALF
MLA_X2 = LANES - ROPE_HALF
_MLA_PAD = MLA_X2 - (HALF + MLA_NOPE - MLA_X1)
KEY_CHUNK = 256
FFN_CHUNKS = ((0, 1024), (1024, 2048), (2048, FFN_HIDDEN))
VMEM_LIMIT = 56 * 1024 * 1024

F32 = jnp.float32
BF16 = jnp.bfloat16


def _const_spec(shape):
    nd = len(shape)
    return pl.BlockSpec(shape, lambda *_: (0,) * nd, pipeline_mode=pl.Buffered(1))


def _layer_spec(arr, layer):
    nd = arr.ndim
    return pl.BlockSpec((None,) + arr.shape[1:], lambda *_: (layer,) + (0,) * (nd - 1),
                        pipeline_mode=pl.Buffered(1))


def _interleave(*gens):
    live = list(gens)
    while live:
        for g in list(live):
            if next(g, _DONE) is _DONE:
                live.remove(g)


_DONE = object()


def _norm_rope_pair(ta, tb, gain_a, gain_b, seg2, dim, cos, sin):
    sq = jnp.concatenate([ta * ta, tb * tb], axis=1).astype(BF16)
    ss = jnp.dot(sq, seg2, preferred_element_type=F32)
    outs = []
    for t, gain, s in ((ta, gain_a, ss[:, 0:LANES]), (tb, gain_b, ss[:, LANES:2 * LANES])):
        rinv = lax.rsqrt(s + EPS * dim)
        u = t * gain
        r = u * cos + pltpu.roll(u, HALF, 1) * sin
        outs.append((r * rinv).astype(BF16))
    return outs


def _row_rms(v, gain):
    ms = jnp.mean(v * v, axis=-1, keepdims=True)
    return v * lax.rsqrt(ms + EPS) * gain


def _prologue_kernel(x_ref, g1_ref, w1_ref, gqn_ref, wuq_ref, gkvn_ref, wuk_ref,
                     wuv_ref, lg_ref, rope_ref, seg_ref, out_ref):
    nsub = PROLOGUE_ROWS // SUB_ROWS
    rows = [slice(r * SUB_ROWS, (r + 1) * SUB_ROWS) for r in range(nsub)]
    hs = {}

    def project(r):
        xn = _row_rms(x_ref[0, rows[r]], g1_ref[...]).astype(BF16)
        hs[r] = jnp.dot(xn, w1_ref[...], preferred_element_type=F32)
        yield

    def finish(r):
        return _prologue_finish(rows[r], hs.pop(r), gqn_ref, wuq_ref, gkvn_ref, wuk_ref,
                                wuv_ref, lg_ref, rope_ref, seg_ref, out_ref)

    _interleave(project(0))
    for r in range(nsub):
        _interleave(finish(r), *([project(r + 1)] if r + 1 < nsub else []))


def _prologue_finish(rs, h, gqn_ref, wuq_ref, gkvn_ref, wuk_ref, wuv_ref, lg_ref,
                     rope_ref, seg_ref, out_ref):
    seg_tile = seg_ref[0]
    seg_half = seg_ref[1]
    mla_t = (rope_ref[0, rs], rope_ref[1, rs])
    dif_t = (rope_ref[2, rs], rope_ref[3, rs])
    gqa_t = (rope_ref[4, rs], rope_ref[5, rs])
    lg = lg_ref[...]

    def tile(a, off, j):
        return a[:, off + j * LANES:off + (j + 1) * LANES]

    def emit(jobs, seg2, dim, tables):
        for (ta, ga, ca), (tb, gb, cb) in zip(jobs[0::2], jobs[1::2]):
            ra, rb = _norm_rope_pair(ta, tb, ga, gb, seg2, dim, *tables)
            out_ref[0, rs, ca:ca + LANES] = ra
            out_ref[0, rs, cb:cb + LANES] = rb
            yield

    cq = _row_rms(h[:, H_CQ:H_CKV], gqn_ref[...]).astype(BF16)
    qa = jnp.dot(cq, wuq_ref[...], preferred_element_type=F32)
    ckv = _row_rms(h[:, H_CKV:H_KR], gkvn_ref[...]).astype(BF16)
    kn = jnp.dot(ckv, wuk_ref[...], preferred_element_type=F32)
    va = jnp.dot(ckv, wuv_ref[...], preferred_element_type=F32)
    kr = h[:, H_KR:H_DQ]
    out_ref[0, rs, O_VA:O_VA + MLA_HEADS * MLA_V] = va.astype(BF16)
    yield

    jobs = [(tile(h, H_DQ, j), lg[2:3], O_QD + j * LANES) for j in range(DIFF_HEADS)]
    jobs += [(tile(h, H_DK, j), lg[3:4], O_KD + j * LANES) for j in range(DIFF_HEADS)]
    yield from emit(jobs, seg_half, DIFF_QK, dif_t)
    out_ref[0, rs, O_VD:O_VD + DIFF_HEADS * DIFF_V] = h[:, H_DV:H_GQ].astype(BF16)

    jobs = [(tile(h, H_GQ, j), lg[4:5], O_QG + j * LANES) for j in range(GQA_GROUP)]
    jobs += [(tile(h, H_GK, 0), lg[5:6], O_KG)]
    yield from emit(jobs, seg_half, GQA_DIM, gqa_t)
    out_ref[0, rs, O_VG:O_END] = h[:, H_GV:H_END].astype(BF16)

    jobs = [(tile(qa, 0, j), lg[0:1], O_QA + j * LANES) for j in range(MLA_HEADS)]
    jobs += [(tile(kn, 0, j) + kr, lg[1:2], O_KA + j * LANES) for j in range(MLA_HEADS)]
    yield from emit(jobs, seg_tile, MLA_QK, mla_t)


def _prologue(layer, x, g1, w1, gqn, wuq, gkvn, wuk, wuv, lane_gain, rope, seg):
    B, S, _ = x.shape
    nrb = S // PROLOGUE_ROWS
    params = (g1, w1, gqn, wuq, gkvn, wuk, wuv, lane_gain)
    return pl.pallas_call(
        _prologue_kernel,
        grid=(nrb, B),
        in_specs=[pl.BlockSpec((1, PROLOGUE_ROWS, D_MODEL), lambda i, b: (b, i, 0))]
        + [_layer_spec(p, layer) for p in params]
        + [pl.BlockSpec((rope.shape[0], PROLOGUE_ROWS, LANES), lambda i, b: (0, i, 0)),
           _const_spec(seg.shape)],
        out_specs=pl.BlockSpec((1, PROLOGUE_ROWS, O_END), lambda i, b: (b, i, 0)),
        out_shape=jax.ShapeDtypeStruct((B, S, O_END), BF16),
        compiler_params=pltpu.CompilerParams(
            dimension_semantics=("arbitrary", "arbitrary"),
            vmem_limit_bytes=VMEM_LIMIT),
        name="prologue",
    )(x, g1, w1, gqn, wuq, gkvn, wuk, wuv, lane_gain, rope, seg)


def _lane_tiles(a):
    return [a[:, j * LANES:(j + 1) * LANES] for j in range(a.shape[1] // LANES)]


def _attention_pipeline(q_ref, groups, o_ref, scratch, combine):
    kt_scr, s_scr, m_scr, p_scr, v_scr = scratch
    seq = q_ref.shape[1]
    nblk = seq // Q_BLOCK
    nchunk = seq // KEY_CHUNK
    units = [(g, i * Q_BLOCK, h) for g in range(len(groups)) for i in range(nblk)
             for h in range(2)]

    def score_pass(g, r0, h):
        q_col, kt_slot = groups[g][h], groups[g][2][h]
        q = q_ref[0, r0:r0 + Q_BLOCK, q_col:q_col + LANES]
        mx = None
        for c in range(nchunk):
            ks = slice(c * KEY_CHUNK, (c + 1) * KEY_CHUNK)
            s = jnp.dot(q, kt_scr[kt_slot, :, ks], preferred_element_type=F32)
            s_scr[h, :, ks] = s
            part = functools.reduce(jnp.maximum, _lane_tiles(s))
            mx = part if mx is None else jnp.maximum(mx, part)
            yield
        m_scr[h] = jnp.broadcast_to(jnp.max(mx, axis=-1, keepdims=True), (Q_BLOCK, LANES))
        yield

    def value_pass(g, h, outs):
        mb = m_scr[h]
        for c in range(nchunk):
            ks = slice(c * KEY_CHUNK, (c + 1) * KEY_CHUNK)
            p = jnp.concatenate([jnp.exp2(t - mb) for t in _lane_tiles(s_scr[h, :, ks])],
                                axis=1)
            p_scr[h, :, ks] = p.astype(BF16)
            yield
        acc = jnp.dot(p_scr[h], v_scr[groups[g][3]], preferred_element_type=F32)
        outs.append(acc[:, 0:LANES] / acc[:, LANES:2 * LANES])
        yield

    _interleave(score_pass(*units[0]))
    outs = []
    for n, (g, r0, h) in enumerate(units):
        nxt = [score_pass(*units[n + 1])] if n + 1 < len(units) else []
        _interleave(value_pass(g, h, outs), *nxt)
        if h == 1:
            oc = groups[g][4]
            o_ref[0, r0:r0 + Q_BLOCK, oc:oc + LANES] = combine(*outs).astype(BF16)
            outs = []


def _pick_halves(o0, o1):
    low = lax.broadcasted_iota(jnp.int32, o0.shape, 1) < HALF
    return jnp.where(low, o0, o1)


def _stage_values(v_ref, v_scr):
    seq = v_ref.shape[1]
    for j in range(v_scr.shape[0]):
        v_scr[j, :, 0:LANES] = v_ref[0, :, j * LANES:(j + 1) * LANES]
        v_scr[j, :, LANES:2 * LANES] = jnp.ones((seq, LANES), BF16)


def _stage_half_keys(k_ref, kt_scr):
    for j in range(kt_scr.shape[0] // 2):
        kt = k_ref[0, :, j * LANES:(j + 1) * LANES].T
        low = (lax.broadcasted_iota(jnp.int32, kt.shape, 0) & HEAD_BLOCK) == 0
        zero = jnp.zeros_like(kt)
        kt_scr[2 * j] = jnp.where(low, kt, zero)
        kt_scr[2 * j + 1] = jnp.where(low, zero, kt)


def _mla_attn_kernel(q_ref, k_ref, v_ref, o_ref, *scratch):
    kt_scr, v_scr = scratch[0], scratch[4]
    for j in range(kt_scr.shape[0]):
        kt_scr[j] = k_ref[0, :, j * LANES:(j + 1) * LANES].T
    _stage_values(v_ref, v_scr)
    groups = [(2 * p * LANES, (2 * p + 1) * LANES, (2 * p, 2 * p + 1), p, p * LANES)
              for p in range(v_scr.shape[0])]
    _attention_pipeline(q_ref, groups, o_ref, scratch, _pick_halves)


def _gqa_attn_kernel(q_ref, k_ref, v_ref, o_ref, *scratch):
    _stage_half_keys(k_ref, scratch[0])
    _stage_values(v_ref, scratch[4])
    groups = [(g * LANES, g * LANES, (0, 1), 0, g * LANES)
              for g in range(q_ref.shape[2] // LANES)]
    _attention_pipeline(q_ref, groups, o_ref, scratch, _pick_halves)


def _diff_attn_kernel(lv_ref, og_ref, q_ref, k_ref, v_ref, o_ref, *scratch, lambda_init):
    _stage_half_keys(k_ref, scratch[0])
    _stage_values(v_ref, scratch[4])
    groups = [(j * LANES, j * LANES, (2 * j, 2 * j + 1), j, j * LANES)
              for j in range(scratch[4].shape[0])]
    lv = lv_ref[...]
    lam = (jnp.exp(jnp.sum(lv[0:1] * lv[1:2], axis=-1, keepdims=True))
           - jnp.exp(jnp.sum(lv[2:3] * lv[3:4], axis=-1, keepdims=True))
           + lambda_init)
    out_gain = og_ref[...] * (1.0 - lambda_init)

    def combine(o0, o1):
        return _row_rms(o0 - lam * o1, out_gain)

    _attention_pipeline(q_ref, groups, o_ref, scratch, combine)


def _attn_scratch(seq, key_slots, value_slots):
    return [pltpu.VMEM((key_slots, LANES, seq), BF16),
            pltpu.VMEM((2, Q_BLOCK, seq), F32),
            pltpu.VMEM((2, Q_BLOCK, LANES), F32),
            pltpu.VMEM((2, Q_BLOCK, seq), BF16),
            pltpu.VMEM((value_slots, seq, 2 * LANES), BF16)]


def _attn_params(grid_rank):
    return pltpu.CompilerParams(dimension_semantics=("arbitrary",) * grid_rank,
                                vmem_limit_bytes=VMEM_LIMIT)


def _mla_attention(qkv):
    B, S, _ = qkv.shape
    qk_w, v_w = 2 * LANES, LANES
    return pl.pallas_call(
        _mla_attn_kernel,
        grid=(B, MLA_HEADS // 2),
        in_specs=[
            pl.BlockSpec((1, S, qk_w), lambda b, p: (b, 0, O_QA // qk_w + p)),
            pl.BlockSpec((1, S, qk_w), lambda b, p: (b, 0, O_KA // qk_w + p)),
            pl.BlockSpec((1, S, v_w), lambda b, p: (b, 0, O_VA // v_w + p)),
        ],
        out_specs=pl.BlockSpec((1, S, v_w), lambda b, p: (b, 0, p)),
        out_shape=jax.ShapeDtypeStruct((B, S, MLA_HEADS * MLA_V), BF16),
        scratch_shapes=_attn_scratch(S, 2, 1),
        compiler_params=_attn_params(2),
        name="mla_attention",
    )(qkv, qkv, qkv)


def _diff_attention(layer, qkv, lvec, out_gain):
    B, S, _ = qkv.shape
    lambda_init = 0.8 - 0.6 * math.exp(-0.3 * layer)
    w = LANES
    return pl.pallas_call(
        functools.partial(_diff_attn_kernel, lambda_init=lambda_init),
        grid=(B, DIFF_HEADS),
        in_specs=[
            _layer_spec(lvec, layer), _layer_spec(out_gain, layer),
            pl.BlockSpec((1, S, w), lambda b, h: (b, 0, O_QD // w + h)),
            pl.BlockSpec((1, S, w), lambda b, h: (b, 0, O_KD // w + h)),
            pl.BlockSpec((1, S, w), lambda b, h: (b, 0, O_VD // w + h)),
        ],
        out_specs=pl.BlockSpec((1, S, w), lambda b, h: (b, 0, h)),
        out_shape=jax.ShapeDtypeStruct((B, S, DIFF_HEADS * DIFF_V), BF16),
        scratch_shapes=_attn_scratch(S, 2, 1),
        compiler_params=_attn_params(2),
        name="diff_attention",
    )(lvec, out_gain, qkv, qkv, qkv)


def _gqa_attention(qkv):
    B, S, _ = qkv.shape
    q_w = LANES
    return pl.pallas_call(
        _gqa_attn_kernel,
        grid=(B, GQA_GROUP),
        in_specs=[
            pl.BlockSpec((1, S, q_w), lambda b, g: (b, 0, O_QG // q_w + g)),
            pl.BlockSpec((1, S, LANES), lambda b, g: (b, 0, O_KG // LANES)),
            pl.BlockSpec((1, S, LANES), lambda b, g: (b, 0, O_VG // LANES)),
        ],
        out_specs=pl.BlockSpec((1, S, q_w), lambda b, g: (b, 0, g)),
        out_shape=jax.ShapeDtypeStruct((B, S, GQA_HEADS * GQA_DIM), BF16),
        scratch_shapes=_attn_scratch(S, 2, 1),
        compiler_params=_attn_params(2),
        name="gqa_attention",
    )(qkv, qkv, qkv)


def _epilogue_kernel(x_ref, oa_ref, ob_ref, oc_ref, wo_ref, g2_ref, wgu_ref, wd_ref, y_ref):
    mix = jnp.concatenate([oa_ref[0], ob_ref[0], oc_ref[0]], axis=1)
    x1 = x_ref[0] + jnp.dot(mix, wo_ref[...], preferred_element_type=F32)
    xn = _row_rms(x1, g2_ref[...]).astype(BF16)
    y = x1
    for lo, hi in FFN_CHUNKS:
        gate = jnp.dot(xn, wgu_ref[:, lo:hi], preferred_element_type=F32)
        up = jnp.dot(xn, wgu_ref[:, FFN_HIDDEN + lo:FFN_HIDDEN + hi],
                     preferred_element_type=F32)
        act = (gate * (1.0 / (1.0 + jnp.exp(-gate))) * up).astype(BF16)
        y = y + jnp.dot(act, wd_ref[lo:hi, :], preferred_element_type=F32)
    y_ref[0] = y


def _epilogue(layer, x, oa, ob, oc, wo, g2, wgu, wd):
    B, S, _ = x.shape
    nrb = S // ROW_BLOCK

    def rows(width):
        return pl.BlockSpec((1, ROW_BLOCK, width), lambda b, i: (b, i, 0))

    return pl.pallas_call(
        _epilogue_kernel,
        grid=(B, nrb),
        in_specs=[rows(D_MODEL), rows(oa.shape[-1]), rows(ob.shape[-1]), rows(oc.shape[-1])]
        + [_layer_spec(p, layer) for p in (wo, g2, wgu, wd)],
        out_specs=rows(D_MODEL),
        out_shape=jax.ShapeDtypeStruct(x.shape, F32),
        compiler_params=pltpu.CompilerParams(
            dimension_semantics=("arbitrary", "arbitrary"),
            vmem_limit_bytes=VMEM_LIMIT),
        name="epilogue",
    )(x, oa, ob, oc, wo, g2, wgu, wd)


def _rope_angles(pos, dim):
    inv = (1.0 / (ROPE_THETA ** (np.arange(0, dim, 2, dtype=np.float32) / dim))
           ).astype(np.float32)
    ang = pos.astype(np.float32)[:, None] * inv[None, :]
    return np.cos(ang), np.sin(ang)


def _rope_tables(seq):
    pos = np.arange(seq, dtype=np.int32)
    cm, sm = _rope_angles(pos, MLA_ROPE)
    cd, sd = _rope_angles(pos, DIFF_QK)
    cr, sr = _rope_angles(pos // GRID_W, GQA_DIM // 2)
    cc, sc = _rope_angles(pos % GRID_W, GQA_DIM // 2)
    one = lambda w: np.ones((seq, w), np.float32)
    zero = lambda w: np.zeros((seq, w), np.float32)
    cat = lambda parts: np.concatenate(parts, axis=-1)
    gap = MLA_X2 - MLA_X1 - ROPE_HALF
    mla = [cat([one(MLA_X1), cm, one(gap), cm]), cat([zero(MLA_X1), -sm, zero(gap), sm])]
    dif = [cat([cd] * 4), cat([-sd, -sd, sd, sd])]
    gqa = [cat([cr, cc] * 4), cat([-sr, -sc, -sr, -sc, sr, sc, sr, sc])]
    return jnp.asarray(np.stack(mla + dif + gqa).astype(np.float32))


def _segment_ones():
    lane = np.arange(2 * LANES)
    same_tile = (lane[:, None] // LANES) == (lane[None, :] // LANES)
    same_head = (lane[:, None] // HEAD_BLOCK) % 2 == (lane[None, :] // HEAD_BLOCK) % 2
    seg = np.stack([same_tile, same_tile & same_head])
    return jnp.asarray(seg.astype(np.float32)).astype(BF16)


def _mla_tile(d, zeros):
    x1, x2 = d[..., MLA_NOPE:MLA_NOPE + ROPE_HALF], d[..., MLA_NOPE + ROPE_HALF:MLA_QK]
    return jnp.concatenate([d[..., 0:MLA_X1], x1, d[..., MLA_X1:MLA_NOPE], zeros(_MLA_PAD), x2],
                           axis=-1)


def _diff_tile(d):
    s = d.shape[:-1]
    return jnp.swapaxes(d.reshape(s + (2, 2, 32)), -3, -2).reshape(s + (LANES,))


def _gqa_tile(d):
    s = d.shape[:-1]
    return jnp.moveaxis(d.reshape(s + (2, 2, 2, 16)), -2, -4).reshape(s + (LANES,))


_GQA_HEAD_ORDER = (0, 3, 1, 4, 2, 5)


def kernel(x, attn_norm, w_in, mla_q_norm, mla_w_uq, mla_kv_norm, mla_w_ukv,
           mla_q_gain, mla_k_gain, diff_q_gain, diff_k_gain, diff_lq1, diff_lk1,
           diff_lq2, diff_lk2, diff_out_gain, gqa_q_gain, gqa_k_gain, w_o,
           ffn_norm, w_gate_up, w_down):
    B, S, _ = x.shape
    L = DEPTH
    rope = _rope_tables(S)
    seg = _segment_ones()
    order = jnp.array(_GQA_HEAD_ORDER)

    wb = w_in.astype(BF16)
    tiles = lambda a, n: a.reshape(L, D_MODEL, n, LANES)
    flat = lambda a: a.reshape(L, D_MODEL, -1)
    place = lambda a, tile_fn: jnp.dot(a, tile_fn(jnp.eye(LANES, dtype=BF16)))
    gq = wb[:, :, 2208:2592].reshape(L, D_MODEL, GQA_HEADS, GQA_DIM)[:, :, order]
    rope_lanes = np.zeros((MLA_ROPE, LANES), np.float32)
    rope_lanes[np.arange(ROPE_HALF), MLA_X1 + np.arange(ROPE_HALF)] = 1.0
    rope_lanes[ROPE_HALF + np.arange(ROPE_HALF), MLA_X2 + np.arange(ROPE_HALF)] = 1.0
    w1 = jnp.concatenate(
        [wb[:, :, 0:640],
         jnp.dot(wb[:, :, 640:672], jnp.asarray(rope_lanes).astype(BF16)),
         flat(place(tiles(wb[:, :, 672:1696], 2 * DIFF_HEADS), _diff_tile)),
         wb[:, :, 1696:2208],
         flat(place(tiles(flat(gq), GQA_GROUP), _gqa_tile)),
         place(wb[:, :, 2592:2720], _gqa_tile), wb[:, :, 2720:2848]],
        axis=-1)
    uq = mla_w_uq.reshape(L, MLA_Q_RANK, MLA_HEADS, MLA_QK)
    wuq = _mla_tile(uq, lambda w: jnp.zeros(uq.shape[:-1] + (w,), F32)
                    ).reshape(L, MLA_Q_RANK, MLA_HEADS * LANES).astype(BF16)
    ukv = mla_w_ukv.reshape(L, MLA_KV_RANK, MLA_HEADS, MLA_NOPE + MLA_V)
    zk = lambda w: jnp.zeros(ukv.shape[:-1] + (w,), F32)
    wuk = jnp.concatenate([ukv[..., 0:MLA_X1], zk(ROPE_HALF), ukv[..., MLA_X1:MLA_NOPE],
                           zk(_MLA_PAD + ROPE_HALF)], axis=-1
                          ).reshape(L, MLA_KV_RANK, MLA_HEADS * LANES).astype(BF16)
    wuv = ukv[..., MLA_NOPE:].reshape(L, MLA_KV_RANK, MLA_HEADS * MLA_V).astype(BF16)
    n_ab = MLA_HEADS * MLA_V + DIFF_HEADS * DIFF_V
    wob = w_o.astype(BF16)
    woc = wob[:, n_ab:].reshape(L, GQA_HEADS, GQA_DIM, D_MODEL)[:, order]
    wo = jnp.concatenate([wob[:, 0:n_ab], woc.reshape(L, GQA_HEADS * GQA_DIM, D_MODEL)],
                         axis=1)
    wgu = w_gate_up.astype(BF16)
    wd = w_down.astype(BF16)

    mla_g = lambda g: _mla_tile(g, lambda w: jnp.zeros((L, w), F32))
    twice = lambda g: jnp.concatenate([g, g], axis=-1)
    lane_gain = jnp.stack(
        [mla_g(mla_q_gain) * LOG2E, mla_g(mla_k_gain) * MLA_QK ** 0.5,
         _diff_tile(twice(diff_q_gain)) * LOG2E,
         _diff_tile(twice(diff_k_gain)) * DIFF_QK ** 0.5,
         _gqa_tile(twice(gqa_q_gain)) * LOG2E,
         _gqa_tile(twice(gqa_k_gain)) * GQA_DIM ** 0.5,
         jnp.zeros((L, LANES), F32), jnp.zeros((L, LANES), F32)], axis=1)
    pad64 = lambda g: jnp.pad(g, ((0, 0), (0, LANES - DIFF_QK)))
    lvec = jnp.stack([pad64(diff_lq1), pad64(diff_lk1), pad64(diff_lq2), pad64(diff_lk2)]
                     + [jnp.zeros((L, LANES), F32)] * 4, axis=1)

    row = lambda g: g[:, None, :]
    for l in range(L):
        qkv = _prologue(l, x, row(attn_norm), w1, row(mla_q_norm), wuq, row(mla_kv_norm),
                        wuk, wuv, lane_gain, rope, seg)
        oa = _mla_attention(qkv)
        ob = _diff_attention(l, qkv, lvec, row(diff_out_gain))
        oc = _gqa_attention(qkv)
        x = _epilogue(l, x, oa, ob, oc, wo, row(ffn_norm), wgu, wd)
    return x
```
